```python
import math
import jax, jax.numpy as jnp
from jax import lax
import numpy as np

D_MODEL = 1024
BATCH = 8
SEQ = 4096
DEPTH = 4

GRID_W = 64
CTX_LEN = 256
HEAD_DIM = 64
N_HEADS_A = 8
N_KV_A = 2
REP_A = N_HEADS_A // N_KV_A
WINDOW = 128
ATTN_BLOCK = 128
N_HEADS_B = 4
RET_DK = 64
RET_DV = 128
RET_CHUNK = 128
N_HEADS_C = 4
DIFF_DK = 64
DIFF_DV = 128
N_BRANCH = 3
BRANCH_W = 512
D_FF = 2816
N_EXPERTS = 8
TOP_K = 2
EXPERT_BLOCK = 512
ROPE_BASE = 10000.0
ROPE_FREQS = HEAD_DIM // 4
NORM_EPS = 1e-6
NEG_INF = -1e30
IN_SIZES = (N_HEADS_A * HEAD_DIM, N_KV_A * HEAD_DIM, N_KV_A * HEAD_DIM,
            N_HEADS_B * RET_DK, N_HEADS_B * RET_DK, N_HEADS_B * RET_DV, N_HEADS_B * RET_DV,
            N_HEADS_C * 2 * DIFF_DK, N_HEADS_C * 2 * DIFF_DK, N_HEADS_C * DIFF_DV,
            N_BRANCH * D_MODEL)
D_IN = 6912

kernel_name = "hybrid_diffusion_gqa_retention_diffattn_moe"


def _rms(x):
    x32 = x.astype(jnp.float32)
    return x32 * lax.rsqrt(jnp.mean(jnp.square(x32), axis=-1, keepdims=True) + NORM_EPS)


def _rmsnorm(x, g):
    return (_rms(x) * g.astype(jnp.float32)).astype(x.dtype)


def _axial_rope_tables(n_tokens):
    rows = n_tokens // GRID_W
    row = jnp.repeat(jnp.arange(rows, dtype=jnp.float32), GRID_W)
    col = jnp.tile(jnp.arange(GRID_W, dtype=jnp.float32), rows)
    inv = 1.0 / (ROPE_BASE ** (jnp.arange(ROPE_FREQS, dtype=jnp.float32) / ROPE_FREQS))
    ang = jnp.stack([row[:, None] * inv, col[:, None] * inv], axis=1)
    return jnp.cos(ang), jnp.sin(ang)


def _rope(x, cos, sin):
    shp = x.shape
    xr = x.reshape(shp[:-1] + (2, 2, ROPE_FREQS))
    bshape = (shp[1],) + (1,) * (x.ndim - 3) + (2, ROPE_FREQS)
    c = cos.reshape(bshape).astype(x.dtype)
    s = sin.reshape(bshape).astype(x.dtype)
    x1, x2 = xr[..., 0, :], xr[..., 1, :]
    return jnp.stack([x1 * c - x2 * s, x2 * c + x1 * s], axis=-2).reshape(shp)


def _sink_softmax(s, sink):
    col = jnp.broadcast_to(sink.astype(jnp.float32)[None, :, :, None, None], s.shape[:-1] + (1,))
    return jax.nn.softmax(jnp.concatenate([s, col], axis=-1), axis=-1)[..., :-1]


def _window_gqa(q, k, v, k_ctx, v_ctx, sink):
    b, n, g, r, d = q.shape
    scale = d ** -0.5
    span = ATTN_BLOCK + 2 * WINDOW
    pad = ((0, 0), (WINDOW, WINDOW), (0, 0), (0, 0))
    kp, vp = jnp.pad(k, pad), jnp.pad(v, pad)
    qi = jnp.arange(ATTN_BLOCK)
    kj = jnp.arange(span)
    rel = kj[None, :] - qi[:, None]
    band = (rel >= 0) & (rel <= 2 * WINDOW)

    def block(i):
        start = i * ATTN_BLOCK
        qs = lax.dynamic_slice_in_dim(q, start, ATTN_BLOCK, axis=1)
        ks = lax.dynamic_slice_in_dim(kp, start, span, axis=1)
        vs = lax.dynamic_slice_in_dim(vp, start, span, axis=1)
        kpos = start - WINDOW + kj
        mask = band & ((kpos >= 0) & (kpos < n))[None, :]
        s_loc = jnp.einsum('bqgrd,bkgd->bgrqk', qs, ks).astype(jnp.float32) * scale
        s_loc = jnp.where(mask, s_loc, NEG_INF)
        s_ctx = jnp.einsum('bqgrd,bkgd->bgrqk', qs, k_ctx).astype(jnp.float32) * scale
        p = _sink_softmax(jnp.concatenate([s_loc, s_ctx], axis=-1), sink).astype(v.dtype)
        return (jnp.einsum('bgrqk,bkgd->bqgrd', p[..., :span], vs)
                + jnp.einsum('bgrqk,bkgd->bqgrd', p[..., span:], v_ctx))

    o = lax.map(block, jnp.arange(n // ATTN_BLOCK))
    return jnp.moveaxis(o, 0, 1).reshape(b, n, g * r * d)


def _ctx_gqa(q, k, v, sink):
    b, n, g, r, d = q.shape
    s = jnp.einsum('bqgrd,bkgd->bgrqk', q, k).astype(jnp.float32) * d ** -0.5
    p = _sink_softmax(s, sink).astype(v.dtype)
    return jnp.einsum('bgrqk,bkgd->bqgrd', p, v).reshape(b, n, g * r * d)


def _retention_scan(q, k, v, log_gamma, s0):
    b, t, h, dk = q.shape
    dv = v.shape[-1]
    c = RET_CHUNK
    n = t // c
    qc = q.reshape(b, n, c, h, dk)
    kc = k.reshape(b, n, c, h, dk)
    vc = v.reshape(b, n, c, h, dv)
    pos = jnp.arange(c, dtype=jnp.float32)
    diff = pos[:, None] - pos[None, :]
    dmat = jnp.where(diff >= 0, jnp.exp(jnp.maximum(diff, 0.0) * log_gamma[:, None, None]), 0.0)
    scores = jnp.einsum('bnchd,bnkhd->bnhck', qc, kc) * dmat
    intra = jnp.einsum('bnhck,bnkhe->bnche', scores, vc)
    k_dec = jnp.exp((c - 1 - pos)[None, :] * log_gamma[:, None])
    kv = jnp.einsum('bnkhd,hk,bnkhe->nbhde', kc, k_dec, vc)
    chunk_dec = jnp.exp(c * log_gamma)[None, :, None, None]

    def step(state, kv_n):
        return state * chunk_dec + kv_n, state

    s_last, s_prev = lax.scan(step, s0, kv)
    q_dec = jnp.exp((pos + 1.0)[None, :] * log_gamma[:, None])
    inter = jnp.einsum('bnchd,hc,nbhde->bnche', qc, q_dec, s_prev)
    return (intra + inter).reshape(b, t, h, dv), s_last


def _retention_bidir(q, k, v, lg_f, lg_b, s0_f, s0_b):
    o_f, s_f = _retention_scan(q, k, v, lg_f, s0_f)
    flip = lambda a: jnp.flip(a, axis=1)
    o_b, s_b = _retention_scan(flip(q), flip(k), flip(v), lg_b, s0_b)
    return o_f + flip(o_b), s_f, s_b


def _ret_out(o, g):
    b, t = o.shape[:2]
    y = _rms(o) * jax.nn.silu(g.astype(jnp.float32)).reshape(o.shape)
    return y.astype(g.dtype).reshape(b, t, -1)


def _diff_attend(q, k, v, lam):
    s = jnp.einsum('bqhcd,bkhcd->bhcqk', q, k).astype(jnp.float32) * q.shape[-1] ** -0.5
    p = jax.nn.softmax(s, axis=-1)
    a = p[:, :, 0] - lam * p[:, :, 1]
    return jnp.einsum('bhqk,bkhe->bqhe', a.astype(v.dtype), v)


def _diff_latent(q, k_all, v_all, lam):
    b, n = q.shape[:2]

    def block(i):
        qs = lax.dynamic_slice_in_dim(q, i * ATTN_BLOCK, ATTN_BLOCK, axis=1)
        return _diff_attend(qs, k_all, v_all, lam)

    o = lax.map(block, jnp.arange(n // ATTN_BLOCK))
    return jnp.moveaxis(o, 0, 1).reshape((b, n) + o.shape[-2:])


def _diff_norm(o, gain, lam_init):
    b, t = o.shape[:2]
    y = _rms(o) * gain.astype(jnp.float32) * (1.0 - lam_init)
    return y.astype(o.dtype).reshape(b, t, -1)


def _split_in(p):
    out, off = [], 0
    for sz in IN_SIZES:
        out.append(p[..., off:off + sz])
        off += sz
    return out


def _merge(outs, gate_logits, w_branch, w_out):
    g = gate_logits.reshape(gate_logits.shape[:2] + (N_BRANCH, D_MODEL))
    y = sum(jax.nn.sigmoid(g[:, :, i]) * (o @ w_branch[i]) for i, o in enumerate(outs))
    return y @ w_out


def _rs(t, *shape):
    return t.reshape(t.shape[:2] + shape)


def _mixer(h, hc, cos, sin, w_in, sink, dec_f, dec_b, lam_p, lam_init, d_gain, w_branch, w_out, ctx_out):
    f32 = jnp.float32
    b = h.shape[0]
    qa, ka, va, qb, kb, vb, gb, qd, kd, vd, gl = _split_in(h @ w_in)
    qa_c, ka_c, va_c, qb_c, kb_c, vb_c, gb_c, qd_c, kd_c, vd_c, gl_c = _split_in(hc @ w_in)
    sink = sink.reshape(N_KV_A, REP_A)
    qa = _rope(_rs(qa, N_KV_A, REP_A, HEAD_DIM), cos, sin)
    ka = _rope(_rs(ka, N_KV_A, HEAD_DIM), cos, sin)
    va = _rs(va, N_KV_A, HEAD_DIM)
    ka_c, va_c = _rs(ka_c, N_KV_A, HEAD_DIM), _rs(va_c, N_KV_A, HEAD_DIM)
    o_a = _window_gqa(qa, ka, va, ka_c, va_c, sink)
    lg_f = jax.nn.log_sigmoid(dec_f.astype(f32))
    lg_b = jax.nn.log_sigmoid(dec_b.astype(f32))
    qb = _rope(_rs(qb, N_HEADS_B, RET_DK), cos, sin).astype(f32)
    kb = _rope(_rs(kb, N_HEADS_B, RET_DK), cos, sin).astype(f32) * RET_DK ** -0.5
    vb = _rs(vb, N_HEADS_B, RET_DV).astype(f32)
    qb_c = _rs(qb_c, N_HEADS_B, RET_DK).astype(f32)
    kb_c = _rs(kb_c, N_HEADS_B, RET_DK).astype(f32) * RET_DK ** -0.5
    vb_c = _rs(vb_c, N_HEADS_B, RET_DV).astype(f32)
    zero = jnp.zeros((b, N_HEADS_B, RET_DK, RET_DV), f32)
    o_bc, s_f, s_b = _retention_bidir(qb_c, kb_c, vb_c, lg_f, lg_b, zero, zero)
    o_b, _, _ = _retention_bidir(qb, kb, vb, lg_f, lg_b, s_f, s_b)
    o_b = _ret_out(o_b, gb)
    lp = lam_p.astype(f32)
    lam = jnp.exp(jnp.sum(lp[0] * lp[1])) - jnp.exp(jnp.sum(lp[2] * lp[3])) + lam_init
    qd = _rope(_rs(qd, N_HEADS_C, 2, DIFF_DK), cos, sin)
    kd = _rope(_rs(kd, N_HEADS_C, 2, DIFF_DK), cos, sin)
    vd = _rs(vd, N_HEADS_C, DIFF_DV)
    kd_c, vd_c = _rs(kd_c, N_HEADS_C, 2, DIFF_DK), _rs(vd_c, N_HEADS_C, DIFF_DV)
    k_all = jnp.concatenate([kd, kd_c], axis=1)
    v_all = jnp.concatenate([vd, vd_c], axis=1)
    o_c = _diff_norm(_diff_latent(qd, k_all, v_all, lam), d_gain, lam_init)
    y = _merge([o_a, o_b, o_c], gl, w_branch, w_out)
    if not ctx_out:
        return y, None
    o_ac = _ctx_gqa(_rs(qa_c, N_KV_A, REP_A, HEAD_DIM), ka_c, va_c, sink)
    o_bc = _ret_out(o_bc, gb_c)
    o_cc = _diff_norm(_diff_attend(_rs(qd_c, N_HEADS_C, 2, DIFF_DK), kd_c, vd_c, lam), d_gain, lam_init)
    yc = _merge([o_ac, o_bc, o_cc], gl_c, w_branch, w_out)
    return y, yc


def _swiglu(h, w1, w3, w2):
    return (jax.nn.silu(h @ w1) * (h @ w3)) @ w2


def _moe(h, w_router, w1, w3, w2):
    t, d = h.shape
    logits = (h @ w_router).astype(jnp.float32)
    top_val, top_idx = lax.top_k(logits, TOP_K)
    gates = jax.nn.softmax(top_val, axis=-1).astype(h.dtype)
    flat_e = top_idx.reshape(-1)
    flat_t = jnp.repeat(jnp.arange(t), TOP_K)
    order = jnp.argsort(flat_e)
    e_s, t_s, g_s = flat_e[order], flat_t[order], gates.reshape(-1)[order]
    counts = jnp.bincount(flat_e, length=N_EXPERTS)
    starts = jnp.cumsum(counts) - counts
    padded = (counts + EXPERT_BLOCK - 1) // EXPERT_BLOCK * EXPERT_BLOCK
    pad_ends = jnp.cumsum(padded)
    dest = (pad_ends - padded)[e_s] + jnp.arange(t * TOP_K) - starts[e_s]
    n_blocks = (t * TOP_K + EXPERT_BLOCK - 1) // EXPERT_BLOCK + N_EXPERTS
    buf = jnp.zeros((n_blocks * EXPERT_BLOCK, d), h.dtype).at[dest].set(h[t_s])
    block_e = jnp.minimum(jnp.searchsorted(pad_ends, jnp.arange(n_blocks) * EXPERT_BLOCK, side='right'),
                          N_EXPERTS - 1)

    def expert_block(args):
        xb, e = args
        return _swiglu(xb, w1[e], w3[e], w2[e])

    out_buf = lax.map(expert_block, (buf.reshape(n_blocks, EXPERT_BLOCK, d), block_e)).reshape(-1, d)
    return jnp.zeros_like(h).at[t_s].add(out_buf[dest] * g_s[:, None])


def setup_inputs(seed: int = 0) -> dict:
    key = jax.random.key(seed)
    ks = jax.random.split(key, 26)
    nrm = lambda k, shape, s: jax.random.normal(k, shape, jnp.float32) * s
    n_dense = (DEPTH + 1) // 2
    n_moe = DEPTH // 2
    D = D_MODEL
    hb = jnp.arange(N_HEADS_B, dtype=jnp.float32)
    ret_logit = jnp.log(1.0 - 2.0 ** (-5.0 - hb)) + (5.0 + hb) * math.log(2.0)
    return {
        "x": nrm(ks[0], (BATCH, SEQ, D), 1.0),
        "c": nrm(ks[1], (BATCH, D), 1.0),
        "ctx": nrm(ks[2], (BATCH, CTX_LEN, D), 1.0),
        "c_ctx": nrm(ks[3], (D,), 1.0),
        "w_mod": nrm(ks[4], (DEPTH, D, 6 * D), 0.5 * D ** -0.5),
        "b_mod": nrm(ks[5], (DEPTH, 6 * D), 0.02),
        "norm_mix": 1.0 + nrm(ks[6], (DEPTH, D), 0.02),
        "norm_ffn": 1.0 + nrm(ks[7], (DEPTH, D), 0.02),
        "norm_final": 1.0 + nrm(ks[8], (D,), 0.02),
        "w_in": nrm(ks[9], (DEPTH, D, D_IN), D ** -0.5),
        "attn_sink": nrm(ks[10], (DEPTH, N_HEADS_A), 0.5),
        "ret_decay_fwd": ret_logit[None, :] + nrm(ks[11], (DEPTH, N_HEADS_B), 0.1),
        "ret_decay_bwd": ret_logit[None, :] + nrm(ks[12], (DEPTH, N_HEADS_B), 0.1),
        "diff_lambda": nrm(ks[13], (DEPTH, 4, DIFF_DK), 0.1),
        "diff_norm": 1.0 + nrm(ks[14], (DEPTH, DIFF_DV), 0.02),
        "w_branch": nrm(ks[15], (DEPTH, N_BRANCH, BRANCH_W, D), BRANCH_W ** -0.5),
        "w_out": nrm(ks[16], (DEPTH, D, D), D ** -0.5),
        "ffn_w1": nrm(ks[17], (n_dense, D, D_FF), D ** -0.5),
        "ffn_w3": nrm(ks[18], (n_dense, D, D_FF), D ** -0.5),
        "ffn_w2": nrm(ks[19], (n_dense, D_FF, D), D_FF ** -0.5),
        "moe_router": nrm(ks[20], (n_moe, D, N_EXPERTS), D ** -0.5),
        "moe_w1": nrm(ks[21], (n_moe, N_EXPERTS, D, D_FF), D ** -0.5),
        "moe_w3": nrm(ks[22], (n_moe, N_EXPERTS, D, D_FF), D ** -0.5),
        "moe_w2": nrm(ks[23], (n_moe, N_EXPERTS, D_FF, D), D_FF ** -0.5),
    }


def reference(x, c, ctx, c_ctx, w_mod, b_mod, norm_mix, norm_ffn, norm_final, w_in, attn_sink,
              ret_decay_fwd, ret_decay_bwd, diff_lambda, diff_norm, w_branch, w_out,
              ffn_w1, ffn_w3, ffn_w2, moe_router, moe_w1, moe_w3, moe_w2):
    b, n, d = x.shape
    cos, sin = _axial_rope_tables(n)
    cs = jax.nn.silu(c)
    ccs = jax.nn.silu(c_ctx)
    xc = ctx
    for l in range(DEPTH):
        ctx_out = l < DEPTH - 1
        lam_init = 0.8 - 0.6 * math.exp(-0.3 * l)
        mod = cs @ w_mod[l] + b_mod[l]
        mod_c = ccs @ w_mod[l] + b_mod[l]
        sh1, sc1, g1, sh2, sc2, g2 = [m[:, None, :] for m in jnp.split(mod, 6, axis=-1)]
        sh1c, sc1c, g1c, sh2c, sc2c, g2c = jnp.split(mod_c, 6, axis=-1)
        h = _rmsnorm(x, norm_mix[l]) * (1 + sc1) + sh1
        hc = _rmsnorm(xc, norm_mix[l]) * (1 + sc1c) + sh1c
        y, yc = _mixer(h, hc, cos, sin, w_in[l], attn_sink[l], ret_decay_fwd[l], ret_decay_bwd[l],
                       diff_lambda[l], lam_init, diff_norm[l], w_branch[l], w_out[l], ctx_out)
        x = x + g1 * y
        tok = (_rmsnorm(x, norm_ffn[l]) * (1 + sc2) + sh2).reshape(-1, d)
        if ctx_out:
            xc = xc + g1c * yc
            hc2 = _rmsnorm(xc, norm_ffn[l]) * (1 + sc2c) + sh2c
            tok = jnp.concatenate([tok, hc2.reshape(-1, d)], axis=0)
        if l % 2 == 0:
            f = _swiglu(tok, ffn_w1[l // 2], ffn_w3[l // 2], ffn_w2[l // 2])
        else:
            f = _moe(tok, moe_router[l // 2], moe_w1[l // 2], moe_w3[l // 2], moe_w2[l // 2])
        x = x + g2 * f[:b * n].reshape(b, n, d)
        if ctx_out:
            xc = xc + g2c * f[b * n:].reshape(xc.shape)
    return _rmsnorm(x, norm_final)
```

```python
import functools
import math

import jax
import jax.numpy as jnp
from jax import lax
from jax.experimental import pallas as pl
from jax.experimental.pallas import tpu as pltpu

F32 = jnp.float32
BF16 = jnp.bfloat16
I32 = jnp.int32

D_MODEL = 1024
GRID_W = 64
HEAD_DIM = 64
N_HEADS_A = 8
N_KV_A = 2
REP_A = N_HEADS_A // N_KV_A
WINDOW = 128
ATTN_BLOCK = 128
N_HEADS_B = 4
RET_CHUNK = 128
N_HEADS_C = 4
N_BRANCH = 3
BRANCH_W = 512
D_FF = 2816
N_EXPERTS = 8
ROPE_BASE = 10000.0
ROPE_FREQS = HEAD_DIM // 4
NORM_EPS = 1e-6
NEG_INF = -1e30

LANES = 128
MOD_ROWS = 16
G_QA, G_KA, G_QB, G_KB, G_QD, G_KD = 0, 4, 6, 8, 10, 14
N_ROPE_GROUPS = 18
G_VA, G_VB, G_GB, G_VD, G_GL = 18, 20, 24, 28, 32
PROJ_W = 56 * LANES
M_SH1, M_SC1, M_G1, M_SH2, M_SC2, M_G2 = range(6)

FF_TILE = 1408
ROW_TILE = 512
VMEM_LIMIT = 56 * 1024 * 1024


def _cparams(sem):
    return pltpu.CompilerParams(dimension_semantics=sem, vmem_limit_bytes=VMEM_LIMIT)


def _sigmoid(x):
    return 1.0 / (1.0 + jnp.exp(-x))


def _nt_dot(a, b):
    return lax.dot_general(a, b, (((1,), (1,)), ((), ())), preferred_element_type=F32)


def _rms_rows(x):
    return x * lax.rsqrt(jnp.mean(x * x, axis=-1, keepdims=True) + NORM_EPS)


def _mod_kernel(c_ref, w_ref, b_ref, o_ref):
    c = c_ref[...]
    s = c * _sigmoid(c)
    o_ref[0] = jnp.dot(s, w_ref[0], preferred_element_type=F32,
                       precision=lax.Precision.HIGHEST) + b_ref[0]


def _modulation(cc, w_mod, b_mod):
    depth, d, n = w_mod.shape
    tn = 1536
    return pl.pallas_call(
        _mod_kernel,
        grid=(depth, n // tn),
        in_specs=[pl.BlockSpec((MOD_ROWS, d), lambda l, j: (0, 0)),
                  pl.BlockSpec((1, d, tn), lambda l, j: (l, 0, j)),
                  pl.BlockSpec((1, 1, tn), lambda l, j: (l, 0, j))],
        out_specs=pl.BlockSpec((1, MOD_ROWS, tn), lambda l, j: (l, 0, j)),
        out_shape=jax.ShapeDtypeStruct((depth, MOD_ROWS, n), F32),
        compiler_params=_cparams(("arbitrary", "arbitrary")),
        name="modulation",
    )(cc, w_mod, b_mod.reshape(depth, 1, n))


def _mod_spec(chunk, nargs):
    if nargs == 1:
        return pl.BlockSpec((MOD_ROWS, D_MODEL), lambda i: (0, chunk))
    if nargs == 2:
        return pl.BlockSpec((MOD_ROWS, D_MODEL), lambda i, j: (0, chunk))
    return pl.BlockSpec((MOD_ROWS, D_MODEL), lambda i, j, *_: (0, chunk))


def _segment(i, n_lat_tiles, tiles_per_batch, nb):
    return jnp.where(i < n_lat_tiles, i // tiles_per_batch, nb)


def _proj_kernel(x_ref, gn_ref, sh_ref, sc_ref, w_ref, cos_ref, sa_ref, sb_ref, o_ref, h_ref, *,
                 n_lat_tiles, tiles_per_batch, nb, groups):
    i = pl.program_id(0)
    j = pl.program_id(1)

    @pl.when(j == 0)
    def _():
        seg = _segment(i, n_lat_tiles, tiles_per_batch, nb)
        sc = sc_ref[pl.ds(seg, 1), :]
        sh = sh_ref[pl.ds(seg, 1), :]
        h = (_rms_rows(x_ref[...]) * gn_ref[...]) * (1.0 + sc) + sh
        h_ref[...] = h.astype(BF16)

    acc = jnp.dot(h_ref[...], w_ref[...], preferred_element_type=F32)
    for g in range(groups):
        cols = slice(g * LANES, (g + 1) * LANES)
        a = acc[:, cols]
        gi = j * groups + g

        @pl.when(gi < N_ROPE_GROUPS)
        def _():
            y = (a * cos_ref[...] + pltpu.roll(a, LANES - 16, 1) * sa_ref[...]
                 + pltpu.roll(a, 16, 1) * sb_ref[...])
            o_ref[:, cols] = y.astype(BF16)

        @pl.when(gi >= N_ROPE_GROUPS)
        def _():
            o_ref[:, cols] = a.astype(BF16)


def _project(xs, gn, mod_l, w, tabs, *, tm, n_tiles, n_lat_tiles, tiles_per_batch, nb):
    tn = 1024
    groups = tn // LANES
    rope_map = lambda i, j: (jnp.where(i < n_lat_tiles, i % tiles_per_batch, tiles_per_batch), 0)
    kern = functools.partial(_proj_kernel, n_lat_tiles=n_lat_tiles, tiles_per_batch=tiles_per_batch,
                             nb=nb, groups=groups)
    return pl.pallas_call(
        kern,
        grid=(n_tiles, PROJ_W // tn),
        in_specs=[pl.BlockSpec((tm, D_MODEL), lambda i, j: (i, 0)),
                  pl.BlockSpec((1, D_MODEL), lambda i, j: (0, 0)),
                  _mod_spec(M_SH1, 2), _mod_spec(M_SC1, 2),
                  pl.BlockSpec((D_MODEL, tn), lambda i, j: (0, j)),
                  pl.BlockSpec((tm, LANES), rope_map),
                  pl.BlockSpec((tm, LANES), rope_map),
                  pl.BlockSpec((tm, LANES), rope_map)],
        out_specs=pl.BlockSpec((tm, tn), lambda i, j: (i, j)),
        out_shape=jax.ShapeDtypeStruct((xs.shape[0], PROJ_W), BF16),
        scratch_shapes=[pltpu.VMEM((tm, D_MODEL), BF16)],
        compiler_params=_cparams(("arbitrary", "arbitrary")),
        name="in_proj",
    )(xs, gn, mod_l, mod_l, w, *tabs)


def _rope_tables(s, tm):
    rows = s // GRID_W
    row = jnp.repeat(jnp.arange(rows, dtype=F32), GRID_W)
    col = jnp.tile(jnp.arange(GRID_W, dtype=F32), rows)
    inv = 1.0 / (ROPE_BASE ** (jnp.arange(ROPE_FREQS, dtype=F32) / ROPE_FREQS))
    ar, ac = row[:, None] * inv, col[:, None] * inv
    z = jnp.zeros_like(ar)
    cos = jnp.concatenate([jnp.cos(ar), jnp.cos(ar), jnp.cos(ac), jnp.cos(ac)], axis=1)
    sa = jnp.concatenate([-jnp.sin(ar), z, -jnp.sin(ac), z], axis=1)
    sb = jnp.concatenate([z, jnp.sin(ar), z, jnp.sin(ac)], axis=1)
    ident = lambda t, v: jnp.concatenate([jnp.tile(t, (1, 2)), jnp.full((tm, LANES), v, F32)], axis=0)
    return ident(cos, 1.0), ident(sa, 0.0), ident(sb, 0.0)


def _proj_weight(w):
    sl = lambda a, b: w[:, a:b]
    sc = HEAD_DIM ** -0.5
    dup = lambda a: jnp.concatenate([sl(a, a + 64), sl(a, a + 64), sl(a + 64, a + 128), sl(a + 64, a + 128)], axis=1)
    parts = [sl(0, 512) * sc, dup(512), sl(768, 1024), sl(1024, 1280) * sc, sl(2304, 2816) * sc,
             sl(2816, 3328), dup(640), sl(1280, 1792), sl(1792, 2304), sl(3328, 3840), sl(3840, 6912)]
    return jnp.concatenate(parts, axis=1).astype(BF16)


def _gqa_kernel(*refs, local, s_len, tq):
    if local:
        (q_ref, kp_ref, kc_ref, kn_ref, kx_ref, vp_ref, vc_ref, vn_ref, vx_ref, sink_ref, _, o_ref) = refs
    else:
        q_ref, kx_ref, vx_ref, sink_ref, _, o_ref = refs
    lo = lax.broadcasted_iota(I32, (tq, LANES), 1) < HEAD_DIM
    if local:
        nk = 3 * tq + kx_ref.shape[0]
        qi = lax.broadcasted_iota(I32, (tq, nk), 0)
        kj = lax.broadcasted_iota(I32, (tq, nk), 1)
        rel = kj - qi
        kpos = pl.program_id(1) * tq - WINDOW + kj
        bad = ((rel < 0) | (rel > 2 * WINDOW) | (kpos < 0) | (kpos >= s_len)) & (kj < 3 * tq)
    for g in range(N_KV_A):
        cols = slice(g * LANES, (g + 1) * LANES)
        if local:
            kcat = jnp.concatenate([kp_ref[:, cols], kc_ref[:, cols], kn_ref[:, cols], kx_ref[:, cols]], axis=0)
            vcat = jnp.concatenate([vp_ref[:, cols], vc_ref[:, cols], vn_ref[:, cols], vx_ref[:, cols]], axis=0)
        else:
            kcat, vcat = kx_ref[:, cols], vx_ref[:, cols]
        qs = []
        for r in range(REP_A):
            h = g * REP_A + r
            qp = q_ref[:, (h // 2) * LANES:(h // 2 + 1) * LANES]
            qs.append(jnp.where(lo if h % 2 == 0 else jnp.logical_not(lo), qp, jnp.zeros_like(qp)))
        s = _nt_dot(jnp.concatenate(qs, axis=0), kcat)
        ps, dens = [], []
        for r in range(REP_A):
            h = g * REP_A + r
            sr = s[r * tq:(r + 1) * tq]
            if local:
                sr = jnp.where(bad, NEG_INF, sr)
            sk = sink_ref[h:h + 1, 0:1]
            m = jnp.maximum(jnp.max(sr, axis=-1, keepdims=True), sk)
            p = jnp.exp(sr - m)
            dens.append(jnp.sum(p, axis=-1, keepdims=True) + jnp.exp(sk - m))
            ps.append(p.astype(BF16))
        o = jnp.dot(jnp.concatenate(ps, axis=0), vcat, preferred_element_type=F32)
        outs = [o[r * tq:(r + 1) * tq] / dens[r] for r in range(REP_A)]
        for pr in range(REP_A // 2):
            pair = jnp.where(lo, outs[2 * pr], outs[2 * pr + 1])
            c0 = (g * (REP_A // 2) + pr) * LANES
            o_ref[:, c0:c0 + LANES] = pair.astype(BF16)


def _gqa(proj, sink_tab, o_prev, *, nb, s_len, c_len, local):
    n_rows = proj.shape[0]
    cblk = nb * s_len // c_len
    kcol, vcol = G_KA // 2, G_VA // 2
    out_shape = jax.ShapeDtypeStruct((n_rows, N_HEADS_A * HEAD_DIM), BF16)
    if local:
        tq = ATTN_BLOCK
        nq = s_len // tq
        row = lambda b, i: b * nq + i
        specs = [pl.BlockSpec((tq, 4 * LANES), lambda b, i: (row(b, i), 0))]
        for col in (kcol, vcol):
            specs += [pl.BlockSpec((tq, 2 * LANES), lambda b, i, col=col: (row(b, jnp.maximum(i - 1, 0)), col)),
                      pl.BlockSpec((tq, 2 * LANES), lambda b, i, col=col: (row(b, i), col)),
                      pl.BlockSpec((tq, 2 * LANES), lambda b, i, col=col: (row(b, jnp.minimum(i + 1, nq - 1)), col)),
                      pl.BlockSpec((c_len, 2 * LANES), lambda b, i, col=col: (cblk + b, col))]
        specs += [pl.BlockSpec((N_HEADS_A, LANES), lambda b, i: (0, 0)),
                  pl.BlockSpec(memory_space=pl.ANY)]
        args = [proj] + [proj] * 8 + [sink_tab, o_prev]
        return pl.pallas_call(
            functools.partial(_gqa_kernel, local=True, s_len=s_len, tq=tq),
            grid=(nb, nq), in_specs=specs,
            out_specs=pl.BlockSpec((tq, 4 * LANES), lambda b, i: (row(b, i), 0)),
            out_shape=out_shape, input_output_aliases={10: 0},
            compiler_params=_cparams(("arbitrary", "arbitrary")), name="gqa_window",
        )(*args)
    specs = [pl.BlockSpec((c_len, 4 * LANES), lambda b: (cblk + b, 0)),
             pl.BlockSpec((c_len, 2 * LANES), lambda b: (cblk + b, kcol)),
             pl.BlockSpec((c_len, 2 * LANES), lambda b: (cblk + b, vcol)),
             pl.BlockSpec((N_HEADS_A, LANES), lambda b: (0, 0)),
             pl.BlockSpec(memory_space=pl.ANY)]
    return pl.pallas_call(
        functools.partial(_gqa_kernel, local=False, s_len=s_len, tq=c_len),
        grid=(nb,), in_specs=specs,
        out_specs=pl.BlockSpec((c_len, 4 * LANES), lambda b: (cblk + b, 0)),
        out_shape=out_shape, input_output_aliases={4: 0},
        compiler_params=_cparams(("arbitrary",)), name="gqa_context",
    )(proj, proj, proj, sink_tab, o_prev)


def _ret_kernel(q_ref, k_ref, v_ref, g_ref, dec_ref, sin_ref, _, o_ref, sout_ref, of_ref, ob_ref, *, n_rows):
    h = pl.program_id(1)
    c = RET_CHUNK
    nch = n_rows // c
    lane = lax.broadcasted_iota(I32, (c, LANES), 1)
    row = lax.broadcasted_iota(I32, (c, LANES), 0)
    qmask = (lane // HEAD_DIM) == (h % 2)

    def log_sigmoid(d):
        x = dec_ref[d, pl.ds(h, 1), :]
        return jnp.minimum(x, 0.0) - jnp.log(1.0 + jnp.exp(-jnp.abs(x)))

    lg_f, lg_b = log_sigmoid(0), log_sigmoid(1)
    diff = row - lane
    rowf = row.astype(F32)
    dmat_f = jnp.where(diff >= 0, jnp.exp(jnp.maximum(diff, 0).astype(F32) * lg_f), 0.0)
    dmat_b = jnp.where(diff <= 0, jnp.exp(jnp.maximum(-diff, 0).astype(F32) * lg_b), 0.0)
    qdec_f = jnp.exp((rowf + 1.0) * lg_f)
    kdec_f = jnp.exp((c - 1.0 - rowf) * lg_f)
    qdec_b = jnp.exp((c - rowf) * lg_b)
    kdec_b = jnp.exp(rowf * lg_b)
    cdec_f = jnp.exp(c * lg_f)
    cdec_b = jnp.exp(c * lg_b)

    def chunk(n, state, dmat, qdec, kdec, cdec, dst_ref):
        rows = pl.ds(pl.multiple_of(n * c, c), c)
        q = q_ref[rows, :]
        q = jnp.where(qmask, q, jnp.zeros_like(q))
        k = k_ref[rows, :]
        v = v_ref[rows, :]
        sc = (_nt_dot(q, k) * dmat).astype(BF16)
        intra = jnp.dot(sc, v, preferred_element_type=F32)
        inter = jnp.dot((q.astype(F32) * qdec).astype(BF16), state.astype(BF16), preferred_element_type=F32)
        dst_ref[rows, :] = intra + inter
        kd = (k.astype(F32) * kdec).T.astype(BF16)
        return state * cdec + jnp.dot(kd, v, preferred_element_type=F32)

    def body(n, carry):
        sf, sb = carry
        sf = chunk(n, sf, dmat_f, qdec_f, kdec_f, cdec_f, of_ref)
        sb = chunk(nch - 1 - n, sb, dmat_b, qdec_b, kdec_b, cdec_b, ob_ref)
        return sf, sb

    sf, sb = lax.fori_loop(0, nch, body, (sin_ref[0, 0, 0], sin_ref[0, 0, 1]))
    sout_ref[0, 0, 0] = sf
    sout_ref[0, 0, 1] = sb
    o = of_ref[...] + ob_ref[...]
    gate = g_ref[...].astype(F32)
    o_ref[...] = (_rms_rows(o) * (gate * _sigmoid(gate))).astype(BF16)


def _retention(proj, dec_tab, s_in, o_prev, *, nb, s_len, c_len, latent):
    n_rows = s_len if latent else c_len
    base = 0 if latent else nb * s_len // c_len
    rb = lambda b: base + b
    col = lambda g0, per_pair: (lambda b, h: (rb(b), g0 + (h // 2 if per_pair else h)))
    state_spec = pl.BlockSpec((1, 1, 2, LANES, LANES), lambda b, h: (b, h, 0, 0, 0))
    specs = [pl.BlockSpec((n_rows, LANES), col(G_QB, True)),
             pl.BlockSpec((n_rows, LANES), col(G_KB, True)),
             pl.BlockSpec((n_rows, LANES), col(G_VB, False)),
             pl.BlockSpec((n_rows, LANES), col(G_GB, False)),
             pl.BlockSpec((2, N_HEADS_B, LANES), lambda b, h: (0, 0, 0)),
             state_spec, pl.BlockSpec(memory_space=pl.ANY)]
    args = [proj, proj, proj, proj, dec_tab, s_in, o_prev]
    aliases = {6: 0}
    return pl.pallas_call(
        functools.partial(_ret_kernel, n_rows=n_rows),
        grid=(nb, N_HEADS_B), in_specs=specs,
        out_specs=[pl.BlockSpec((n_rows, LANES), lambda b, h: (rb(b), h)), state_spec],
        out_shape=[jax.ShapeDtypeStruct((proj.shape[0], N_HEADS_B * LANES), BF16),
                   jax.ShapeDtypeStruct(s_in.shape, F32)],
        scratch_shapes=[pltpu.VMEM((n_rows, LANES), F32), pltpu.VMEM((n_rows, LANES), F32)],
        input_output_aliases=aliases,
        compiler_params=_cparams(("arbitrary", "arbitrary")),
        name="retention_latent" if latent else "retention_context",
    )(*args)


def _diff_kernel(*refs, latent, lam_init):
    if latent:
        q_ref, kl_ref, kx_ref, vl_ref, vx_ref, lam_ref, gain_ref, _, o_ref = refs
    else:
        q_ref, kx_ref, vx_ref, lam_ref, gain_ref, _, o_ref = refs
    lp = lam_ref[...]
    lam = (jnp.exp(jnp.sum(lp[0:1] * lp[1:2], axis=-1, keepdims=True))
           - jnp.exp(jnp.sum(lp[2:3] * lp[3:4], axis=-1, keepdims=True)) + lam_init)
    q = q_ref[...]
    lo = lax.broadcasted_iota(I32, q.shape, 1) < HEAD_DIM
    outs = []
    for comp in range(2):
        qc = jnp.where(lo if comp == 0 else jnp.logical_not(lo), q, jnp.zeros_like(q))
        sx = _nt_dot(qc, kx_ref[...])
        m = jnp.max(sx, axis=-1, keepdims=True)
        if latent:
            sl = _nt_dot(qc, kl_ref[...])
            m = jnp.maximum(m, jnp.max(sl, axis=-1, keepdims=True))
        px = jnp.exp(sx - m)
        den = jnp.sum(px, axis=-1, keepdims=True)
        o = jnp.dot(px.astype(BF16), vx_ref[...], preferred_element_type=F32)
        if latent:
            pq = jnp.exp(sl - m)
            den = den + jnp.sum(pq, axis=-1, keepdims=True)
            o = o + jnp.dot(pq.astype(BF16), vl_ref[...], preferred_element_type=F32)
        outs.append(o / den)
    o = outs[0] - lam * outs[1]
    o_ref[...] = (_rms_rows(o) * gain_ref[...] * (1.0 - lam_init)).astype(BF16)


def _diff_attention(proj, lam_p, gain, o_prev, *, nb, s_len, c_len, lam_init, latent):
    cblk = nb * s_len // c_len
    out_shape = jax.ShapeDtypeStruct((proj.shape[0], N_HEADS_C * LANES), BF16)
    small = [pl.BlockSpec((4, HEAD_DIM), lambda *_: (0, 0)), pl.BlockSpec((1, LANES), lambda *_: (0, 0))]
    if latent:
        tq = 256
        nq = s_len // tq
        specs = [pl.BlockSpec((tq, LANES), lambda b, h, i: (b * nq + i, G_QD + h)),
                 pl.BlockSpec((s_len, LANES), lambda b, h, i: (b, G_KD + h)),
                 pl.BlockSpec((c_len, LANES), lambda b, h, i: (cblk + b, G_KD + h)),
                 pl.BlockSpec((s_len, LANES), lambda b, h, i: (b, G_VD + h)),
                 pl.BlockSpec((c_len, LANES), lambda b, h, i: (cblk + b, G_VD + h))] + small
        specs.append(pl.BlockSpec(memory_space=pl.ANY))
        return pl.pallas_call(
            functools.partial(_diff_kernel, latent=True, lam_init=lam_init),
            grid=(nb, N_HEADS_C, nq), in_specs=specs,
            out_specs=pl.BlockSpec((tq, LANES), lambda b, h, i: (b * nq + i, h)),
            out_shape=out_shape, input_output_aliases={7: 0},
            compiler_params=_cparams(("arbitrary", "arbitrary", "arbitrary")), name="diff_latent",
        )(proj, proj, proj, proj, proj, lam_p, gain, o_prev)
    specs = [pl.BlockSpec((c_len, LANES), lambda b, h: (cblk + b, G_QD + h)),
             pl.BlockSpec((c_len, LANES), lambda b, h: (cblk + b, G_KD + h)),
             pl.BlockSpec((c_len, LANES), lambda b, h: (cblk + b, G_VD + h))] + small
    specs.append(pl.BlockSpec(memory_space=pl.ANY))
    return pl.pallas_call(
        functools.partial(_diff_kernel, latent=False, lam_init=lam_init),
        grid=(nb, N_HEADS_C), in_specs=specs,
        out_specs=pl.BlockSpec((c_len, LANES), lambda b, h: (cblk + b, h)),
        out_shape=out_shape, input_output_aliases={5: 0},
        compiler_params=_cparams(("arbitrary", "arbitrary")), name="diff_context",
    )(proj, proj, proj, lam_p, gain, o_prev)


def _merge_kernel(oa_ref, ob_ref, oc_ref, ga_ref, gb_ref, gc_ref, wb_ref, wo_ref, x_ref, g1_ref, sh_ref, sc_ref,
                  gn_ref, xo_ref, tok_ref, *, n_lat_tiles, tiles_per_batch, nb):
    i = pl.program_id(0)
    seg = _segment(i, n_lat_tiles, tiles_per_batch, nb)
    y = None
    for k, (o_ref, gl_ref) in enumerate(((oa_ref, ga_ref), (ob_ref, gb_ref), (oc_ref, gc_ref))):
        t = _sigmoid(gl_ref[...].astype(F32)) * jnp.dot(o_ref[...], wb_ref[k], preferred_element_type=F32)
        y = t if y is None else y + t
    z = jnp.dot(y.astype(BF16), wo_ref[...], preferred_element_type=F32)
    xn = x_ref[...] + g1_ref[pl.ds(seg, 1), :] * z
    xo_ref[...] = xn
    tok = (_rms_rows(xn) * gn_ref[...]) * (1.0 + sc_ref[pl.ds(seg, 1), :]) + sh_ref[pl.ds(seg, 1), :]
    tok_ref[...] = tok.astype(tok_ref.dtype)


def _merge(oa, ob, oc, proj, wb, wo, xs, mod_l, gn, *, tm, n_tiles, n_lat_tiles, tiles_per_batch, nb, tok_dtype):
    row = lambda w: pl.BlockSpec((tm, w), lambda i: (i, 0))
    gl = lambda k: pl.BlockSpec((tm, D_MODEL), lambda i: (i, G_GL // 8 + k))
    return pl.pallas_call(
        functools.partial(_merge_kernel, n_lat_tiles=n_lat_tiles, tiles_per_batch=tiles_per_batch, nb=nb),
        grid=(n_tiles,),
        in_specs=[row(BRANCH_W), row(BRANCH_W), row(BRANCH_W), gl(0), gl(1), gl(2),
                  pl.BlockSpec((N_BRANCH, BRANCH_W, D_MODEL), lambda i: (0, 0, 0)),
                  pl.BlockSpec((D_MODEL, D_MODEL), lambda i: (0, 0)),
                  row(D_MODEL), _mod_spec(M_G1, 1), _mod_spec(M_SH2, 1), _mod_spec(M_SC2, 1),
                  pl.BlockSpec((1, D_MODEL), lambda i: (0, 0))],
        out_specs=[row(D_MODEL), row(D_MODEL)],
        out_shape=[jax.ShapeDtypeStruct(xs.shape, F32), jax.ShapeDtypeStruct((n_tiles * tm, D_MODEL), tok_dtype)],
        input_output_aliases={8: 0},
        compiler_params=_cparams(("arbitrary",)), name="merge",
    )(oa, ob, oc, proj, proj, proj, wb, wo, xs, mod_l, mod_l, mod_l, gn)


def _ffn_kernel(be_ref, nu_ref, *refs, dense, n_lat_tiles, tiles_per_batch, nb):
    if dense:
        t_ref, w1_ref, w3_ref, w2_ref, x_ref, g2_ref, o_ref, acc_ref = refs
    else:
        t_ref, w1_ref, w3_ref, w2_ref, o_ref, acc_ref = refs
    i = pl.program_id(0)
    f = pl.program_id(1)
    last = pl.num_programs(1) - 1
    used = i < nu_ref[0]

    @pl.when(used)
    def _():
        t = t_ref[...].astype(BF16)
        h1 = jnp.dot(t, w1_ref[0], preferred_element_type=F32)
        h3 = jnp.dot(t, w3_ref[0], preferred_element_type=F32)
        a = ((h1 * _sigmoid(h1)) * h3).astype(BF16)
        part = jnp.dot(a, w2_ref[0], preferred_element_type=F32)

        @pl.when(f == 0)
        def _():
            acc_ref[...] = part

        @pl.when(f > 0)
        def _():
            acc_ref[...] += part

    @pl.when(f == last)
    def _():
        if dense:
            seg = _segment(i, n_lat_tiles, tiles_per_batch, nb)
            o_ref[...] = x_ref[...] + g2_ref[pl.ds(seg, 1), :] * acc_ref[...]
        else:
            o_ref[...] = jnp.where(used, acc_ref[...], 0.0)


def _ffn(tok, w1, w3, w2, block_expert, n_used, *, tm, n_blocks, xs=None, mod_l=None,
         n_lat_tiles=0, tiles_per_batch=1, nb=0):
    dense = xs is not None
    nf = D_FF // FF_TILE
    specs = [pl.BlockSpec((tm, D_MODEL), lambda i, f, be, nu: (i, 0)),
             pl.BlockSpec((1, D_MODEL, FF_TILE), lambda i, f, be, nu: (be[i], 0, f)),
             pl.BlockSpec((1, D_MODEL, FF_TILE), lambda i, f, be, nu: (be[i], 0, f)),
             pl.BlockSpec((1, FF_TILE, D_MODEL), lambda i, f, be, nu: (be[i], f, 0))]
    args = [tok, w1, w3, w2]
    aliases = {}
    if dense:
        specs += [pl.BlockSpec((tm, D_MODEL), lambda i, f, be, nu: (i, 0)), _mod_spec(M_G2, 4)]
        args += [xs, mod_l]
        aliases = {6: 0}
        out_rows = xs.shape[0]
    else:
        out_rows = tok.shape[0]
    return pl.pallas_call(
        functools.partial(_ffn_kernel, dense=dense, n_lat_tiles=n_lat_tiles, tiles_per_batch=tiles_per_batch, nb=nb),
        grid_spec=pltpu.PrefetchScalarGridSpec(
            num_scalar_prefetch=2, grid=(n_blocks, nf), in_specs=specs,
            out_specs=pl.BlockSpec((tm, D_MODEL), lambda i, f, be, nu: (i, 0)),
            scratch_shapes=[pltpu.VMEM((tm, D_MODEL), F32)]),
        out_shape=jax.ShapeDtypeStruct((out_rows, D_MODEL), F32),
        input_output_aliases=aliases,
        compiler_params=_cparams(("arbitrary", "arbitrary")),
        name="ffn_dense" if dense else "ffn_experts",
    )(block_expert, n_used, *args)


def _router_kernel(tok_ref, wr_ref, ids_ref, gate_ref, cnt_ref, base_ref, *, tr):
    i = pl.program_id(0)

    @pl.when(i == 0)
    def _():
        base_ref[...] = jnp.zeros_like(base_ref)

    logits = _nt_dot(wr_ref[...], tok_ref[...].astype(BF16))
    eidx = lax.broadcasted_iota(I32, logits.shape, 0)
    m1 = jnp.max(logits, axis=0, keepdims=True)
    i1 = jnp.min(jnp.where(logits == m1, eidx, N_EXPERTS), axis=0, keepdims=True)
    rest = jnp.where(eidx == i1, -jnp.inf, logits)
    m2 = jnp.max(rest, axis=0, keepdims=True)
    i2 = jnp.min(jnp.where(rest == m2, eidx, N_EXPERTS), axis=0, keepdims=True)
    e = jnp.exp(m2 - m1)
    sel1 = eidx == i1
    sel2 = eidx == i2
    both = jnp.where(sel1 | sel2, 1.0, 0.0)
    before = lax.broadcasted_iota(I32, (tr, tr), 0) < lax.broadcasted_iota(I32, (tr, tr), 1)
    prefix = jnp.dot(both.astype(BF16), jnp.where(before, 1.0, 0.0).astype(BF16), preferred_element_type=F32)
    pos = base_ref[:, 0:1] + prefix
    r1 = jnp.sum(jnp.where(sel1, pos, 0.0), axis=0, keepdims=True)
    r2 = jnp.sum(jnp.where(sel2, pos, 0.0), axis=0, keepdims=True)
    ids_ref[0:1, :] = i1
    ids_ref[1:2, :] = i2
    ids_ref[2:3, :] = r1.astype(I32)
    ids_ref[3:4, :] = r2.astype(I32)
    ids_ref[4:8, :] = jnp.zeros((4, tr), I32)
    gate_ref[0:1, :] = 1.0 / (1.0 + e)
    gate_ref[1:2, :] = e / (1.0 + e)
    gate_ref[2:8, :] = jnp.zeros((6, tr), F32)
    base_ref[...] = base_ref[...] + jnp.sum(both, axis=1, keepdims=True)
    cnt_ref[...] = base_ref[...].astype(I32)


def _router(tok, wr, *, n_rows, tr):
    return pl.pallas_call(
        functools.partial(_router_kernel, tr=tr),
        grid=(n_rows // tr,),
        in_specs=[pl.BlockSpec((tr, D_MODEL), lambda i: (i, 0)),
                  pl.BlockSpec((N_EXPERTS, D_MODEL), lambda i: (0, 0))],
        out_specs=[pl.BlockSpec((8, tr), lambda i: (0, i)), pl.BlockSpec((8, tr), lambda i: (0, i)),
                   pl.BlockSpec((N_EXPERTS, LANES), lambda i: (0, 0))],
        out_shape=[jax.ShapeDtypeStruct((8, n_rows), I32), jax.ShapeDtypeStruct((8, n_rows), F32),
                   jax.ShapeDtypeStruct((N_EXPERTS, LANES), I32)],
        scratch_shapes=[pltpu.VMEM((N_EXPERTS, LANES), F32)],
        compiler_params=_cparams(("arbitrary",)), name="router",
    )(tok, wr)


def _gather_kernel(src_ref, nu_ref, tok_hbm, o_ref, sem, *, bs):
    i = pl.program_id(0)
    used = i < nu_ref[0]

    @pl.when(used)
    def _():
        def issue(r, carry):
            t = src_ref[i * bs + r]
            pltpu.make_async_copy(tok_hbm.at[pl.ds(t, 1), :], o_ref.at[pl.ds(r, 1), :], sem).start()
            return carry

        lax.fori_loop(0, bs, issue, 0)
        pltpu.make_async_copy(tok_hbm.at[pl.ds(0, bs), :], o_ref, sem).wait()

    @pl.when(jnp.logical_not(used))
    def _():
        o_ref[...] = jnp.zeros_like(o_ref)


def _dispatch(tok, src, n_used, *, bs, n_blocks):
    return pl.pallas_call(
        functools.partial(_gather_kernel, bs=bs),
        grid_spec=pltpu.PrefetchScalarGridSpec(
            num_scalar_prefetch=2, grid=(n_blocks,),
            in_specs=[pl.BlockSpec(memory_space=pl.ANY)],
            out_specs=pl.BlockSpec((bs, D_MODEL), lambda i, src, nu: (i, 0)),
            scratch_shapes=[pltpu.SemaphoreType.DMA(())]),
        out_shape=jax.ShapeDtypeStruct((n_blocks * bs, D_MODEL), tok.dtype),
        compiler_params=_cparams(("arbitrary",)), name="moe_dispatch",
    )(src, n_used, tok)


def _combine_kernel(d0_ref, d1_ref, y_hbm, x_ref, gate_ref, g2_ref, gn_ref, o_ref, buf_ref, sem, *,
                    tc, n_lat_tiles, tiles_per_batch, nb, final):
    i = pl.program_id(0)

    def issue(r, carry):
        pltpu.make_async_copy(y_hbm.at[pl.ds(d0_ref[i * tc + r], 1), :], buf_ref.at[0, pl.ds(r, 1), :], sem).start()
        pltpu.make_async_copy(y_hbm.at[pl.ds(d1_ref[i * tc + r], 1), :], buf_ref.at[1, pl.ds(r, 1), :], sem).start()
        return carry

    lax.fori_loop(0, tc, issue, 0)
    pltpu.make_async_copy(y_hbm.at[pl.ds(0, tc), :], buf_ref.at[0], sem).wait()
    pltpu.make_async_copy(y_hbm.at[pl.ds(0, tc), :], buf_ref.at[1], sem).wait()
    seg = _segment(i, n_lat_tiles, tiles_per_batch, nb)
    gates = gate_ref[...]
    f = gates[:, 0:1] * buf_ref[0] + gates[:, 1:2] * buf_ref[1]
    xo = x_ref[...] + g2_ref[pl.ds(seg, 1), :] * f
    if final:
        xo = _rms_rows(xo) * gn_ref[...]
    o_ref[...] = xo


def _combine(dest0, dest1, y, xs, gates, mod_l, gn, *, tc, n_tiles, n_lat_tiles, tiles_per_batch, nb, final):
    out_rows = n_tiles * tc
    return pl.pallas_call(
        functools.partial(_combine_kernel, tc=tc, n_lat_tiles=n_lat_tiles, tiles_per_batch=tiles_per_batch,
                          nb=nb, final=final),
        grid_spec=pltpu.PrefetchScalarGridSpec(
            num_scalar_prefetch=2, grid=(n_tiles,),
            in_specs=[pl.BlockSpec(memory_space=pl.ANY),
                      pl.BlockSpec((tc, D_MODEL), lambda i, a, b: (i, 0)),
                      pl.BlockSpec((tc, 2), lambda i, a, b: (i, 0)),
                      _mod_spec(M_G2, 3),
                      pl.BlockSpec((1, D_MODEL), lambda i, a, b: (0, 0))],
            out_specs=pl.BlockSpec((tc, D_MODEL), lambda i, a, b: (i, 0)),
            scratch_shapes=[pltpu.VMEM((2, tc, D_MODEL), F32), pltpu.SemaphoreType.DMA(())]),
        out_shape=jax.ShapeDtypeStruct((out_rows, D_MODEL), F32),
        compiler_params=_cparams(("arbitrary",)), name="moe_combine",
    )(dest0, dest1, y, xs, gates, mod_l, gn)


def _moe(tok, xs, mod_l, w_router, w1, w3, w2, gn_final, *, n_rows, tm, n_lat_tiles, nb, s_len, final):
    bs = ROW_TILE
    ids, gates, counts = _router(tok, w_router.T.astype(BF16), n_rows=n_rows, tr=tm)
    counts = counts[:, 0]
    padded = (counts + bs - 1) // bs * bs
    pad_ends = jnp.cumsum(padded)
    starts = pad_ends - padded
    dest0 = starts[ids[0]] + ids[2]
    dest1 = starts[ids[1]] + ids[3]
    n_blocks = (2 * n_rows + bs - 1) // bs + N_EXPERTS
    block_start = jnp.arange(n_blocks, dtype=I32) * bs
    block_expert = jnp.minimum(jnp.sum((pad_ends[None, :] <= block_start[:, None]).astype(I32), axis=1),
                               N_EXPERTS - 1)
    n_used = (pad_ends[-1:] // bs).astype(I32)
    tid = jnp.arange(n_rows, dtype=I32)
    src = jnp.zeros((n_blocks * bs,), I32).at[dest0].set(tid).at[dest1].set(tid)
    buf = _dispatch(tok, src, n_used, bs=bs, n_blocks=n_blocks)
    y = _ffn(buf, w1, w3, w2, block_expert, n_used, tm=bs, n_blocks=n_blocks)
    tc = 256
    return _combine(dest0, dest1, y, xs, gates[0:2].T, mod_l, gn_final, tc=tc, n_tiles=n_rows // tc,
                    n_lat_tiles=n_lat_tiles * (tm // tc), tiles_per_batch=s_len // tc, nb=nb, final=final)


def kernel(x, c, ctx, c_ctx, w_mod, b_mod, norm_mix, norm_ffn, norm_final, w_in, attn_sink, ret_decay_fwd,
           ret_decay_bwd, diff_lambda, diff_norm, w_branch, w_out, ffn_w1, ffn_w3, ffn_w2, moe_router, moe_w1,
           moe_w3, moe_w2):
    nb, s_len, d = x.shape
    c_len = ctx.shape[1]
    depth = w_in.shape[0]
    assert d == D_MODEL and nb < MOD_ROWS and s_len % 256 == 0 and s_len % c_len == 0 and c_len % RET_CHUNK == 0
    n_lat, n_ctx = nb * s_len, nb * c_len
    tm = ROW_TILE
    assert s_len % tm == 0 and n_ctx % tm == 0
    n_lat_tiles, n_all_tiles, tiles_per_batch = n_lat // tm, (n_lat + n_ctx) // tm, s_len // tm
    tile_kw = dict(tm=tm, n_lat_tiles=n_lat_tiles, tiles_per_batch=tiles_per_batch, nb=nb)

    cc = jnp.zeros((MOD_ROWS, d), F32).at[:nb].set(c).at[nb].set(c_ctx)
    mod = _modulation(cc, w_mod, b_mod)
    tabs = _rope_tables(s_len, tm)
    xs = jnp.concatenate([x.reshape(n_lat, d), ctx.reshape(n_ctx, d)], axis=0)
    one_block = jnp.zeros((n_all_tiles,), I32)
    row = lambda v: v.reshape(1, -1).astype(F32)

    for l in range(depth):
        ctx_out = l < depth - 1
        lam_init = 0.8 - 0.6 * math.exp(-0.3 * l)
        n_tiles = n_all_tiles if ctx_out else n_lat_tiles
        proj = _project(xs, row(norm_mix[l]), mod[l], _proj_weight(w_in[l]), tabs, n_tiles=n_all_tiles, **tile_kw)

        sink_tab = jnp.broadcast_to(attn_sink[l].astype(F32)[:, None], (N_HEADS_A, LANES))
        dec_tab = jnp.broadcast_to(jnp.stack([ret_decay_fwd[l], ret_decay_bwd[l]]).astype(F32)[:, :, None],
                                   (2, N_HEADS_B, LANES))
        lam_p, gain = diff_lambda[l].astype(F32), row(diff_norm[l])
        kw = dict(nb=nb, s_len=s_len, c_len=c_len)
        zero_state = jnp.zeros((nb, N_HEADS_B, 2, LANES, LANES), F32)
        oa, ob, oc = (jnp.zeros((n_lat + n_ctx, BRANCH_W), BF16) for _ in range(N_BRANCH))
        ob, state = _retention(proj, dec_tab, zero_state, ob, latent=False, **kw)
        ob, _ = _retention(proj, dec_tab, state, ob, latent=True, **kw)
        if ctx_out:
            oa = _gqa(proj, sink_tab, oa, local=False, **kw)
            oc = _diff_attention(proj, lam_p, gain, oc, lam_init=lam_init, latent=False, **kw)
        oa = _gqa(proj, sink_tab, oa, local=True, **kw)
        oc = _diff_attention(proj, lam_p, gain, oc, lam_init=lam_init, latent=True, **kw)

        is_moe = l % 2 == 1
        xs, tok = _merge(oa, ob, oc, proj, w_branch[l].astype(BF16), w_out[l].astype(BF16), xs, mod[l],
                         row(norm_ffn[l]), n_tiles=n_tiles, tok_dtype=F32 if is_moe else BF16, **tile_kw)
        final = l == depth - 1
        if is_moe:
            e = l // 2
            xs = _moe(tok, xs, mod[l], moe_router[e], moe_w1[e].astype(BF16), moe_w3[e].astype(BF16),
                      moe_w2[e].astype(BF16), row(norm_final), n_rows=n_tiles * tm, tm=tm,
                      n_lat_tiles=n_lat_tiles, nb=nb, s_len=s_len, final=final)
        else:
            e = l // 2
            xs = _ffn(tok, ffn_w1[e:e + 1].astype(BF16), ffn_w3[e:e + 1].astype(BF16), ffn_w2[e:e + 1].astype(BF16),
                      one_block, jnp.full((1,), n_tiles, I32), n_blocks=n_tiles, xs=xs, mod_l=mod[l], **tile_kw)
            assert not final
    return xs[:n_lat].reshape(nb, s_len, d)
```

```python
import functools
import math

import jax
import jax.numpy as jnp
from jax import lax
from jax.experimental import pallas as pl
from jax.experimental.pallas import tpu as pltpu

F32 = jnp.float32
BF16 = jnp.bfloat16
I32 = jnp.int32

D_MODEL = 1024
GRID_W = 64
HEAD_DIM = 64
N_HEADS_A = 8
N_KV_A = 2
REP_A = N_HEADS_A // N_KV_A
WINDOW = 128
ATTN_BLOCK = 128
N_HEADS_B = 4
RET_CHUNK = 256
N_HEADS_C = 4
N_BRANCH = 3
BRANCH_W = 512
D_FF = 2816
N_EXPERTS = 8
ROPE_BASE = 10000.0
ROPE_FREQS = HEAD_DIM // 4
NORM_EPS = 1e-6
NEG_INF = -1e30
LOG2E = math.log2(math.e)

LANES = 128
MOD_ROWS = 16
G_QA, G_KA, G_QB, G_KB, G_QD, G_KD = 0, 4, 6, 8, 10, 14
N_ROPE_GROUPS = 18
G_VA, G_VB, G_GB, G_VD, G_GL = 18, 20, 24, 28, 32
PROJ_W = 56 * LANES
M_SH1, M_SC1, M_G1, M_SH2, M_SC2, M_G2 = range(6)

FF_TILE = 1408
ROW_TILE = 512
DIFF_Q_TILE = 256
VMEM_LIMIT = 56 * 1024 * 1024


def _cparams(sem):
    return pltpu.CompilerParams(dimension_semantics=sem, vmem_limit_bytes=VMEM_LIMIT)


def _sigmoid(x):
    return 1.0 / (1.0 + jnp.exp(-x))


def _nt_dot(a, b):
    return lax.dot_general(a, b, (((1,), (1,)), ((), ())), preferred_element_type=F32)


def _rms_rows(x):
    return x * lax.rsqrt(jnp.mean(x * x, axis=-1, keepdims=True) + NORM_EPS)


def _mod_kernel(c_ref, w_ref, b_ref, o_ref):
    c = c_ref[...]
    s = c * _sigmoid(c)
    o_ref[0] = jnp.dot(s, w_ref[0], preferred_element_type=F32,
                       precision=lax.Precision.HIGHEST) + b_ref[0]


def _modulation(cc, w_mod, b_mod):
    depth, d, n = w_mod.shape
    tn = 1536
    return pl.pallas_call(
        _mod_kernel,
        grid=(depth, n // tn),
        in_specs=[pl.BlockSpec((MOD_ROWS, d), lambda l, j: (0, 0)),
                  pl.BlockSpec((1, d, tn), lambda l, j: (l, 0, j)),
                  pl.BlockSpec((1, 1, tn), lambda l, j: (l, 0, j))],
        out_specs=pl.BlockSpec((1, MOD_ROWS, tn), lambda l, j: (l, 0, j)),
        out_shape=jax.ShapeDtypeStruct((depth, MOD_ROWS, n), F32),
        compiler_params=_cparams(("arbitrary", "arbitrary")),
        name="modulation",
    )(cc, w_mod, b_mod.reshape(depth, 1, n))


def _mod_spec(chunk, nargs):
    if nargs == 1:
        return pl.BlockSpec((MOD_ROWS, D_MODEL), lambda i: (0, chunk))
    if nargs == 2:
        return pl.BlockSpec((MOD_ROWS, D_MODEL), lambda i, j: (0, chunk))
    return pl.BlockSpec((MOD_ROWS, D_MODEL), lambda i, j, *_: (0, chunk))


def _segment(i, n_lat_tiles, tiles_per_batch, nb):
    return jnp.where(i < n_lat_tiles, i // tiles_per_batch, nb)


def _proj_kernel(x_ref, gn_ref, sh_ref, sc_ref, w_ref, cos_ref, sa_ref, sb_ref, o_ref, h_ref, *,
                 n_lat_tiles, tiles_per_batch, nb, groups):
    i = pl.program_id(0)
    j = pl.program_id(1)

    @pl.when(j == 0)
    def _():
        seg = _segment(i, n_lat_tiles, tiles_per_batch, nb)
        sc = sc_ref[pl.ds(seg, 1), :]
        sh = sh_ref[pl.ds(seg, 1), :]
        h = (_rms_rows(x_ref[...]) * gn_ref[...]) * (1.0 + sc) + sh
        h_ref[...] = h.astype(BF16)

    acc = jnp.dot(h_ref[...], w_ref[...], preferred_element_type=F32)
    for g in range(groups):
        cols = slice(g * LANES, (g + 1) * LANES)
        a = acc[:, cols]
        gi = j * groups + g

        @pl.when(gi < N_ROPE_GROUPS)
        def _():
            y = (a * cos_ref[...] + pltpu.roll(a, LANES - 16, 1) * sa_ref[...]
                 + pltpu.roll(a, 16, 1) * sb_ref[...])
            o_ref[:, cols] = y.astype(BF16)

        @pl.when(gi >= N_ROPE_GROUPS)
        def _():
            o_ref[:, cols] = a.astype(BF16)


def _project(xs, gn, mod_l, w, tabs, *, tm, n_tiles, n_lat_tiles, tiles_per_batch, nb):
    tn = 1024
    groups = tn // LANES
    rope_map = lambda i, j: (jnp.where(i < n_lat_tiles, i % tiles_per_batch, tiles_per_batch), 0)
    kern = functools.partial(_proj_kernel, n_lat_tiles=n_lat_tiles, tiles_per_batch=tiles_per_batch,
                             nb=nb, groups=groups)
    return pl.pallas_call(
        kern,
        grid=(n_tiles, PROJ_W // tn),
        in_specs=[pl.BlockSpec((tm, D_MODEL), lambda i, j: (i, 0)),
                  pl.BlockSpec((1, D_MODEL), lambda i, j: (0, 0)),
                  _mod_spec(M_SH1, 2), _mod_spec(M_SC1, 2),
                  pl.BlockSpec((D_MODEL, tn), lambda i, j: (0, j)),
                  pl.BlockSpec((tm, LANES), rope_map),
                  pl.BlockSpec((tm, LANES), rope_map),
                  pl.BlockSpec((tm, LANES), rope_map)],
        out_specs=pl.BlockSpec((tm, tn), lambda i, j: (i, j)),
        out_shape=jax.ShapeDtypeStruct((xs.shape[0], PROJ_W), BF16),
        scratch_shapes=[pltpu.VMEM((tm, D_MODEL), BF16)],
        compiler_params=_cparams(("arbitrary", "arbitrary")),
        name="in_proj",
    )(xs, gn, mod_l, mod_l, w, *tabs)


def _rope_tables(s, tm):
    rows = s // GRID_W
    row = jnp.repeat(jnp.arange(rows, dtype=F32), GRID_W)
    col = jnp.tile(jnp.arange(GRID_W, dtype=F32), rows)
    inv = 1.0 / (ROPE_BASE ** (jnp.arange(ROPE_FREQS, dtype=F32) / ROPE_FREQS))
    ar, ac = row[:, None] * inv, col[:, None] * inv
    z = jnp.zeros_like(ar)
    cos = jnp.concatenate([jnp.cos(ar), jnp.cos(ar), jnp.cos(ac), jnp.cos(ac)], axis=1)
    sa = jnp.concatenate([-jnp.sin(ar), z, -jnp.sin(ac), z], axis=1)
    sb = jnp.concatenate([z, jnp.sin(ar), z, jnp.sin(ac)], axis=1)
    ident = lambda t, v: jnp.concatenate([jnp.tile(t, (1, 2)), jnp.full((tm, LANES), v, F32)], axis=0)
    return ident(cos, 1.0), ident(sa, 0.0), ident(sb, 0.0)


def _proj_weight(w):
    sl = lambda a, b: w[:, a:b]
    sc = HEAD_DIM ** -0.5
    dup = lambda a: jnp.concatenate([sl(a, a + 64), sl(a, a + 64), sl(a + 64, a + 128), sl(a + 64, a + 128)], axis=1)
    parts = [sl(0, 512) * sc, dup(512), sl(768, 1024), sl(1024, 1280) * sc, sl(2304, 2816) * (sc * LOG2E),
             sl(2816, 3328), dup(640), sl(1280, 1792), sl(1792, 2304), sl(3328, 3840), sl(3840, 6912)]
    return jnp.concatenate(parts, axis=1).astype(BF16)


def _gqa_kernel(*refs, local, s_len, tq):
    if local:
        (q_ref, kp_ref, kc_ref, kn_ref, kx_ref, vp_ref, vc_ref, vn_ref, vx_ref, sink_ref, _, o_ref) = refs
    else:
        q_ref, kx_ref, vx_ref, sink_ref, _, o_ref = refs
    lo = lax.broadcasted_iota(I32, (tq, LANES), 1) < HEAD_DIM
    if local:
        nk = 3 * tq + kx_ref.shape[0]
        qi = lax.broadcasted_iota(I32, (tq, nk), 0)
        kj = lax.broadcasted_iota(I32, (tq, nk), 1)
        rel = kj - qi
        kpos = pl.program_id(1) * tq - WINDOW + kj
        bad = ((rel < 0) | (rel > 2 * WINDOW) | (kpos < 0) | (kpos >= s_len)) & (kj < 3 * tq)
    for g in range(N_KV_A):
        cols = slice(g * LANES, (g + 1) * LANES)
        if local:
            kcat = jnp.concatenate([kp_ref[:, cols], kc_ref[:, cols], kn_ref[:, cols], kx_ref[:, cols]], axis=0)
            vcat = jnp.concatenate([vp_ref[:, cols], vc_ref[:, cols], vn_ref[:, cols], vx_ref[:, cols]], axis=0)
        else:
            kcat, vcat = kx_ref[:, cols], vx_ref[:, cols]
        qs = []
        for r in range(REP_A):
            h = g * REP_A + r
            qp = q_ref[:, (h // 2) * LANES:(h // 2 + 1) * LANES]
            qs.append(jnp.where(lo if h % 2 == 0 else jnp.logical_not(lo), qp, jnp.zeros_like(qp)))
        s = _nt_dot(jnp.concatenate(qs, axis=0), kcat)
        ps, dens = [], []
        for r in range(REP_A):
            h = g * REP_A + r
            sr = s[r * tq:(r + 1) * tq]
            if local:
                sr = jnp.where(bad, NEG_INF, sr)
            sk = sink_ref[h:h + 1, 0:1]
            m = jnp.maximum(jnp.max(sr, axis=-1, keepdims=True), sk)
            p = jnp.exp(sr - m)
            dens.append(jnp.sum(p, axis=-1, keepdims=True) + jnp.exp(sk - m))
            ps.append(p.astype(BF16))
        o = jnp.dot(jnp.concatenate(ps, axis=0), vcat, preferred_element_type=F32)
        outs = [o[r * tq:(r + 1) * tq] / dens[r] for r in range(REP_A)]
        for pr in range(REP_A // 2):
            pair = jnp.where(lo, outs[2 * pr], outs[2 * pr + 1])
            c0 = (g * (REP_A // 2) + pr) * LANES
            o_ref[:, c0:c0 + LANES] = pair.astype(BF16)


def _gqa(proj, sink_tab, o_prev, *, nb, s_len, c_len, local):
    n_rows = proj.shape[0]
    cblk = nb * s_len // c_len
    kcol, vcol = G_KA // 2, G_VA // 2
    out_shape = jax.ShapeDtypeStruct((n_rows, N_HEADS_A * HEAD_DIM), BF16)
    if local:
        tq = ATTN_BLOCK
        nq = s_len // tq
        row = lambda b, i: b * nq + i
        specs = [pl.BlockSpec((tq, 4 * LANES), lambda b, i: (row(b, i), 0))]
        for col in (kcol, vcol):
            specs += [pl.BlockSpec((tq, 2 * LANES), lambda b, i, col=col: (row(b, jnp.maximum(i - 1, 0)), col)),
                      pl.BlockSpec((tq, 2 * LANES), lambda b, i, col=col: (row(b, i), col)),
                      pl.BlockSpec((tq, 2 * LANES), lambda b, i, col=col: (row(b, jnp.minimum(i + 1, nq - 1)), col)),
                      pl.BlockSpec((c_len, 2 * LANES), lambda b, i, col=col: (cblk + b, col))]
        specs += [pl.BlockSpec((N_HEADS_A, LANES), lambda b, i: (0, 0)),
                  pl.BlockSpec(memory_space=pl.ANY)]
        args = [proj] + [proj] * 8 + [sink_tab, o_prev]
        return pl.pallas_call(
            functools.partial(_gqa_kernel, local=True, s_len=s_len, tq=tq),
            grid=(nb, nq), in_specs=specs,
            out_specs=pl.BlockSpec((tq, 4 * LANES), lambda b, i: (row(b, i), 0)),
            out_shape=out_shape, input_output_aliases={10: 0},
            compiler_params=_cparams(("arbitrary", "arbitrary")), name="gqa_window",
        )(*args)
    specs = [pl.BlockSpec((c_len, 4 * LANES), lambda b: (cblk + b, 0)),
             pl.BlockSpec((c_len, 2 * LANES), lambda b: (cblk + b, kcol)),
             pl.BlockSpec((c_len, 2 * LANES), lambda b: (cblk + b, vcol)),
             pl.BlockSpec((N_HEADS_A, LANES), lambda b: (0, 0)),
             pl.BlockSpec(memory_space=pl.ANY)]
    return pl.pallas_call(
        functools.partial(_gqa_kernel, local=False, s_len=s_len, tq=c_len),
        grid=(nb,), in_specs=specs,
        out_specs=pl.BlockSpec((c_len, 4 * LANES), lambda b: (cblk + b, 0)),
        out_shape=out_shape, input_output_aliases={4: 0},
        compiler_params=_cparams(("arbitrary",)), name="gqa_context",
    )(proj, proj, proj, sink_tab, o_prev)


def _ret_kernel(q_ref, k_ref, v_ref, g_ref, dec_ref, sin_ref, _, o_ref, sout_ref, of_ref, ob_ref, *, n_rows):
    h = pl.program_id(1)
    c = RET_CHUNK
    nch = n_rows // c
    lane = lax.broadcasted_iota(I32, (c, LANES), 1)
    row = lax.broadcasted_iota(I32, (c, LANES), 0)
    qmask = (lane // HEAD_DIM) == (h % 2)

    def log_sigmoid(d):
        x = dec_ref[d, pl.ds(h, 1), :]
        return jnp.minimum(x, 0.0) - jnp.log(1.0 + jnp.exp(-jnp.abs(x)))

    lg_f, lg_b = log_sigmoid(0), log_sigmoid(1)
    diff = lax.broadcasted_iota(I32, (c, c), 0) - lax.broadcasted_iota(I32, (c, c), 1)
    rowf = row.astype(F32)
    dmat_f = jnp.where(diff >= 0, jnp.exp(jnp.maximum(diff, 0).astype(F32) * lg_f[:, 0:1]), 0.0)
    dmat_b = jnp.where(diff <= 0, jnp.exp(jnp.maximum(-diff, 0).astype(F32) * lg_b[:, 0:1]), 0.0)
    qdec_f = jnp.exp((rowf + 1.0) * lg_f)
    kdec_f = jnp.exp((c - 1.0 - rowf) * lg_f)
    qdec_b = jnp.exp((c - rowf) * lg_b)
    kdec_b = jnp.exp(rowf * lg_b)
    cdec_f = jnp.exp(c * lg_f)
    cdec_b = jnp.exp(c * lg_b)

    def chunk(n, state, dmat, qdec, kdec, cdec, dst_ref):
        rows = pl.ds(pl.multiple_of(n * c, c), c)
        q = q_ref[rows, :]
        q = jnp.where(qmask, q, jnp.zeros_like(q))
        k = k_ref[rows, :]
        v = v_ref[rows, :]
        sc = (_nt_dot(q, k) * dmat).astype(BF16)
        intra = jnp.dot(sc, v, preferred_element_type=F32)
        inter = jnp.dot((q.astype(F32) * qdec).astype(BF16), state.astype(BF16), preferred_element_type=F32)
        dst_ref[rows, :] = intra + inter
        kd = (k.astype(F32) * kdec).T.astype(BF16)
        return state * cdec + jnp.dot(kd, v, preferred_element_type=F32)

    def body(n, carry):
        sf, sb = carry
        sf = chunk(n, sf, dmat_f, qdec_f, kdec_f, cdec_f, of_ref)
        sb = chunk(nch - 1 - n, sb, dmat_b, qdec_b, kdec_b, cdec_b, ob_ref)
        return sf, sb

    sf, sb = lax.fori_loop(0, nch, body, (sin_ref[0, 0, 0], sin_ref[0, 0, 1]), unroll=2)
    sout_ref[0, 0, 0] = sf
    sout_ref[0, 0, 1] = sb
    o = of_ref[...] + ob_ref[...]
    gate = g_ref[...].astype(F32)
    o_ref[...] = (_rms_rows(o) * (gate * _sigmoid(gate))).astype(BF16)


def _retention(proj, dec_tab, s_in, o_prev, *, nb, s_len, c_len, latent):
    n_rows = s_len if latent else c_len
    base = 0 if latent else nb * s_len // c_len
    rb = lambda b: base + b
    col = lambda g0, per_pair: (lambda b, h: (rb(b), g0 + (h // 2 if per_pair else h)))
    state_spec = pl.BlockSpec((1, 1, 2, LANES, LANES), lambda b, h: (b, h, 0, 0, 0))
    specs = [pl.BlockSpec((n_rows, LANES), col(G_QB, True)),
             pl.BlockSpec((n_rows, LANES), col(G_KB, True)),
             pl.BlockSpec((n_rows, LANES), col(G_VB, False)),
             pl.BlockSpec((n_rows, LANES), col(G_GB, False)),
             pl.BlockSpec((2, N_HEADS_B, LANES), lambda b, h: (0, 0, 0)),
             state_spec, pl.BlockSpec(memory_space=pl.ANY)]
    args = [proj, proj, proj, proj, dec_tab, s_in, o_prev]
    aliases = {6: 0}
    return pl.pallas_call(
        functools.partial(_ret_kernel, n_rows=n_rows),
        grid=(nb, N_HEADS_B), in_specs=specs,
        out_specs=[pl.BlockSpec((n_rows, LANES), lambda b, h: (rb(b), h)), state_spec],
        out_shape=[jax.ShapeDtypeStruct((proj.shape[0], N_HEADS_B * LANES), BF16),
                   jax.ShapeDtypeStruct(s_in.shape, F32)],
        scratch_shapes=[pltpu.VMEM((n_rows, LANES), F32), pltpu.VMEM((n_rows, LANES), F32)],
        input_output_aliases=aliases,
        compiler_params=_cparams(("arbitrary", "arbitrary")),
        name="retention_latent" if latent else "retention_context",
    )(*args)


def _diff_kernel(*refs, latent, lam_init):
    if latent:
        q_ref, kl_ref, kx_ref, vl_ref, vx_ref, lam_ref, gain_ref, _, o_ref, k_scr, v_scr = refs
    else:
        q_ref, kx_ref, vx_ref, lam_ref, gain_ref, _, o_ref, k_scr, v_scr = refs
    nl = kl_ref.shape[0] if latent else 0

    def fill():
        if latent:
            k_scr[0:nl, :] = kl_ref[...]
            v_scr[0:nl, 0:LANES] = vl_ref[...]
        k_scr[nl:, :] = kx_ref[...]
        v_scr[nl:, 0:LANES] = vx_ref[...]
        v_scr[:, LANES:] = jnp.ones((v_scr.shape[0], LANES), BF16)

    if latent:
        pl.when(pl.program_id(2) == 0)(fill)
    else:
        fill()
    lp = lam_ref[...]
    lam = (jnp.exp(jnp.sum(lp[0:1] * lp[1:2], axis=-1, keepdims=True))
           - jnp.exp(jnp.sum(lp[2:3] * lp[3:4], axis=-1, keepdims=True)) + lam_init)
    q = q_ref[...]
    lo = lax.broadcasted_iota(I32, q.shape, 1) < HEAD_DIM
    zero = jnp.zeros_like(q)
    outs = []
    for comp in range(2):
        qc = jnp.where(lo, q, zero) if comp == 0 else jnp.where(lo, zero, q)
        s = _nt_dot(qc, k_scr[...])
        p = jnp.exp2(s - jnp.max(s, axis=-1, keepdims=True)).astype(BF16)
        o = jnp.dot(p, v_scr[...], preferred_element_type=F32)
        outs.append(o[:, :LANES] / o[:, LANES:])
    o = outs[0] - lam * outs[1]
    o_ref[...] = (_rms_rows(o) * gain_ref[...] * (1.0 - lam_init)).astype(BF16)


def _diff_attention(proj, lam_p, gain, o_prev, *, nb, s_len, c_len, lam_init, latent):
    cblk = nb * s_len // c_len
    out_shape = jax.ShapeDtypeStruct((proj.shape[0], N_HEADS_C * LANES), BF16)
    small = [pl.BlockSpec((4, HEAD_DIM), lambda *_: (0, 0)), pl.BlockSpec((1, LANES), lambda *_: (0, 0))]
    if latent:
        tq = min(DIFF_Q_TILE, s_len)
        nq = s_len // tq
        specs = [pl.BlockSpec((tq, LANES), lambda b, h, i: (b * nq + i, G_QD + h)),
                 pl.BlockSpec((s_len, LANES), lambda b, h, i: (b, G_KD + h)),
                 pl.BlockSpec((c_len, LANES), lambda b, h, i: (cblk + b, G_KD + h)),
                 pl.BlockSpec((s_len, LANES), lambda b, h, i: (b, G_VD + h)),
                 pl.BlockSpec((c_len, LANES), lambda b, h, i: (cblk + b, G_VD + h))] + small
        specs.append(pl.BlockSpec(memory_space=pl.ANY))
        return pl.pallas_call(
            functools.partial(_diff_kernel, latent=True, lam_init=lam_init),
            grid=(nb, N_HEADS_C, nq), in_specs=specs,
            out_specs=pl.BlockSpec((tq, LANES), lambda b, h, i: (b * nq + i, h)),
            out_shape=out_shape, input_output_aliases={7: 0},
            scratch_shapes=[pltpu.VMEM((s_len + c_len, LANES), BF16), pltpu.VMEM((s_len + c_len, 2 * LANES), BF16)],
            compiler_params=_cparams(("arbitrary", "arbitrary", "arbitrary")), name="diff_latent",
        )(proj, proj, proj, proj, proj, lam_p, gain, o_prev)
    specs = [pl.BlockSpec((c_len, LANES), lambda b, h: (cblk + b, G_QD + h)),
             pl.BlockSpec((c_len, LANES), lambda b, h: (cblk + b, G_KD + h)),
             pl.BlockSpec((c_len, LANES), lambda b, h: (cblk + b, G_VD + h))] + small
    specs.append(pl.BlockSpec(memory_space=pl.ANY))
    return pl.pallas_call(
        functools.partial(_diff_kernel, latent=False, lam_init=lam_init),
        grid=(nb, N_HEADS_C), in_specs=specs,
        out_specs=pl.BlockSpec((c_len, LANES), lambda b, h: (cblk + b, h)),
        out_shape=out_shape, input_output_aliases={5: 0},
        scratch_shapes=[pltpu.VMEM((c_len, LANES), BF16), pltpu.VMEM((c_len, 2 * LANES), BF16)],
        compiler_params=_cparams(("arbitrary", "arbitrary")), name="diff_context",
    )(proj, proj, proj, lam_p, gain, o_prev)


def _merge_kernel(oa_ref, ob_ref, oc_ref, ga_ref, gb_ref, gc_ref, wb_ref, wo_ref, x_ref, g1_ref, sh_ref, sc_ref,
                  gn_ref, xo_ref, tok_ref, *, n_lat_tiles, tiles_per_batch, nb):
    i = pl.program_id(0)
    seg = _segment(i, n_lat_tiles, tiles_per_batch, nb)
    y = None
    for k, (o_ref, gl_ref) in enumerate(((oa_ref, ga_ref), (ob_ref, gb_ref), (oc_ref, gc_ref))):
        t = _sigmoid(gl_ref[...].astype(F32)) * jnp.dot(o_ref[...], wb_ref[k], preferred_element_type=F32)
        y = t if y is None else y + t
    z = jnp.dot(y.astype(BF16), wo_ref[...], preferred_element_type=F32)
    xn = x_ref[...] + g1_ref[pl.ds(seg, 1), :] * z
    xo_ref[...] = xn
    tok = (_rms_rows(xn) * gn_ref[...]) * (1.0 + sc_ref[pl.ds(seg, 1), :]) + sh_ref[pl.ds(seg, 1), :]
    tok_ref[...] = tok.astype(tok_ref.dtype)


def _merge(oa, ob, oc, proj, wb, wo, xs, mod_l, gn, *, tm, n_tiles, n_lat_tiles, tiles_per_batch, nb, tok_dtype):
    row = lambda w: pl.BlockSpec((tm, w), lambda i: (i, 0))
    gl = lambda k: pl.BlockSpec((tm, D_MODEL), lambda i: (i, G_GL // 8 + k))
    return pl.pallas_call(
        functools.partial(_merge_kernel, n_lat_tiles=n_lat_tiles, tiles_per_batch=tiles_per_batch, nb=nb),
        grid=(n_tiles,),
        in_specs=[row(BRANCH_W), row(BRANCH_W), row(BRANCH_W), gl(0), gl(1), gl(2),
                  pl.BlockSpec((N_BRANCH, BRANCH_W, D_MODEL), lambda i: (0, 0, 0)),
                  pl.BlockSpec((D_MODEL, D_MODEL), lambda i: (0, 0)),
                  row(D_MODEL), _mod_spec(M_G1, 1), _mod_spec(M_SH2, 1), _mod_spec(M_SC2, 1),
                  pl.BlockSpec((1, D_MODEL), lambda i: (0, 0))],
        out_specs=[row(D_MODEL), row(D_MODEL)],
        out_shape=[jax.ShapeDtypeStruct(xs.shape, F32), jax.ShapeDtypeStruct((n_tiles * tm, D_MODEL), tok_dtype)],
        input_output_aliases={8: 0},
        compiler_params=_cparams(("arbitrary",)), name="merge",
    )(oa, ob, oc, proj, proj, proj, wb, wo, xs, mod_l, mod_l, mod_l, gn)


def _ffn_kernel(be_ref, nu_ref, *refs, dense, n_lat_tiles, tiles_per_batch, nb):
    if dense:
        t_ref, w1_ref, w3_ref, w2_ref, x_ref, g2_ref, o_ref, acc_ref = refs
    else:
        t_ref, w1_ref, w3_ref, w2_ref, o_ref, acc_ref = refs
    i = pl.program_id(0)
    f = pl.program_id(1)
    last = pl.num_programs(1) - 1
    used = i < nu_ref[0]

    @pl.when(used)
    def _():
        t = t_ref[...].astype(BF16)
        h1 = jnp.dot(t, w1_ref[0], preferred_element_type=F32)
        h3 = jnp.dot(t, w3_ref[0], preferred_element_type=F32)
        a = ((h1 * _sigmoid(h1)) * h3).astype(BF16)
        part = jnp.dot(a, w2_ref[0], preferred_element_type=F32)

        @pl.when(f == 0)
        def _():
            acc_ref[...] = part

        @pl.when(f > 0)
        def _():
            acc_ref[...] += part

    @pl.when(f == last)
    def _():
        if dense:
            seg = _segment(i, n_lat_tiles, tiles_per_batch, nb)
            o_ref[...] = x_ref[...] + g2_ref[pl.ds(seg, 1), :] * acc_ref[...]
        else:
            o_ref[...] = jnp.where(used, acc_ref[...], 0.0)


def _ffn(tok, w1, w3, w2, block_expert, n_used, *, tm, n_blocks, xs=None, mod_l=None,
         n_lat_tiles=0, tiles_per_batch=1, nb=0):
    dense = xs is not None
    nf = D_FF // FF_TILE
    specs = [pl.BlockSpec((tm, D_MODEL), lambda i, f, be, nu: (i, 0)),
             pl.BlockSpec((1, D_MODEL, FF_TILE), lambda i, f, be, nu: (be[i], 0, f)),
             pl.BlockSpec((1, D_MODEL, FF_TILE), lambda i, f, be, nu: (be[i], 0, f)),
             pl.BlockSpec((1, FF_TILE, D_MODEL), lambda i, f, be, nu: (be[i], f, 0))]
    args = [tok, w1, w3, w2]
    aliases = {}
    if dense:
        specs += [pl.BlockSpec((tm, D_MODEL), lambda i, f, be, nu: (i, 0)), _mod_spec(M_G2, 4)]
        args += [xs, mod_l]
        aliases = {6: 0}
        out_rows = xs.shape[0]
    else:
        out_rows = tok.shape[0]
    return pl.pallas_call(
        functools.partial(_ffn_kernel, dense=dense, n_lat_tiles=n_lat_tiles, tiles_per_batch=tiles_per_batch, nb=nb),
        grid_spec=pltpu.PrefetchScalarGridSpec(
            num_scalar_prefetch=2, grid=(n_blocks, nf), in_specs=specs,
            out_specs=pl.BlockSpec((tm, D_MODEL), lambda i, f, be, nu: (i, 0)),
            scratch_shapes=[pltpu.VMEM((tm, D_MODEL), F32)]),
        out_shape=jax.ShapeDtypeStruct((out_rows, D_MODEL), F32),
        input_output_aliases=aliases,
        compiler_params=_cparams(("arbitrary", "arbitrary")),
        name="ffn_dense" if dense else "ffn_experts",
    )(block_expert, n_used, *args)


def _router_kernel(tok_ref, wr_ref, ids_ref, gate_ref, cnt_ref, base_ref, *, tr):
    i = pl.program_id(0)

    @pl.when(i == 0)
    def _():
        base_ref[...] = jnp.zeros_like(base_ref)

    logits = _nt_dot(wr_ref[...], tok_ref[...].astype(BF16))
    eidx = lax.broadcasted_iota(I32, logits.shape, 0)
    m1 = jnp.max(logits, axis=0, keepdims=True)
    i1 = jnp.min(jnp.where(logits == m1, eidx, N_EXPERTS), axis=0, keepdims=True)
    rest = jnp.where(eidx == i1, -jnp.inf, logits)
    m2 = jnp.max(rest, axis=0, keepdims=True)
    i2 = jnp.min(jnp.where(rest == m2, eidx, N_EXPERTS), axis=0, keepdims=True)
    e = jnp.exp(m2 - m1)
    sel1 = eidx == i1
    sel2 = eidx == i2
    both = jnp.where(sel1 | sel2, 1.0, 0.0)
    before = lax.broadcasted_iota(I32, (tr, tr), 0) < lax.broadcasted_iota(I32, (tr, tr), 1)
    prefix = jnp.dot(both.astype(BF16), jnp.where(before, 1.0, 0.0).astype(BF16), preferred_element_type=F32)
    pos = base_ref[:, 0:1] + prefix
    r1 = jnp.sum(jnp.where(sel1, pos, 0.0), axis=0, keepdims=True)
    r2 = jnp.sum(jnp.where(sel2, pos, 0.0), axis=0, keepdims=True)
    ids_ref[0:1, :] = i1
    ids_ref[1:2, :] = i2
    ids_ref[2:3, :] = r1.astype(I32)
    ids_ref[3:4, :] = r2.astype(I32)
    ids_ref[4:8, :] = jnp.zeros((4, tr), I32)
    gate_ref[0:1, :] = 1.0 / (1.0 + e)
    gate_ref[1:2, :] = e / (1.0 + e)
    gate_ref[2:8, :] = jnp.zeros((6, tr), F32)
    base_ref[...] = base_ref[...] + jnp.sum(both, axis=1, keepdims=True)
    cnt_ref[...] = base_ref[...].astype(I32)


def _router(tok, wr, *, n_rows, tr):
    return pl.pallas_call(
        functools.partial(_router_kernel, tr=tr),
        grid=(n_rows // tr,),
        in_specs=[pl.BlockSpec((tr, D_MODEL), lambda i: (i, 0)),
                  pl.BlockSpec((N_EXPERTS, D_MODEL), lambda i: (0, 0))],
        out_specs=[pl.BlockSpec((8, tr), lambda i: (0, i)), pl.BlockSpec((8, tr), lambda i: (0, i)),
                   pl.BlockSpec((N_EXPERTS, LANES), lambda i: (0, 0))],
        out_shape=[jax.ShapeDtypeStruct((8, n_rows), I32), jax.ShapeDtypeStruct((8, n_rows), F32),
                   jax.ShapeDtypeStruct((N_EXPERTS, LANES), I32)],
        scratch_shapes=[pltpu.VMEM((N_EXPERTS, LANES), F32)],
        compiler_params=_cparams(("arbitrary",)), name="router",
    )(tok, wr)


def _gather_kernel(src_ref, nu_ref, tok_hbm, o_ref, sem, *, bs):
    i = pl.program_id(0)
    used = i < nu_ref[0]

    @pl.when(used)
    def _():
        def issue(r, carry):
            t = src_ref[i * bs + r]
            pltpu.make_async_copy(tok_hbm.at[pl.ds(t, 1), :], o_ref.at[pl.ds(r, 1), :], sem).start()
            return carry

        lax.fori_loop(0, bs, issue, 0, unroll=8)
        pltpu.make_async_copy(tok_hbm.at[pl.ds(0, bs), :], o_ref, sem).wait()

    @pl.when(jnp.logical_not(used))
    def _():
        o_ref[...] = jnp.zeros_like(o_ref)


def _dispatch(tok, src, n_used, *, bs, n_blocks):
    return pl.pallas_call(
        functools.partial(_gather_kernel, bs=bs),
        grid_spec=pltpu.PrefetchScalarGridSpec(
            num_scalar_prefetch=2, grid=(n_blocks,),
            in_specs=[pl.BlockSpec(memory_space=pl.ANY)],
            out_specs=pl.BlockSpec((bs, D_MODEL), lambda i, src, nu: (i, 0)),
            scratch_shapes=[pltpu.SemaphoreType.DMA(())]),
        out_shape=jax.ShapeDtypeStruct((n_blocks * bs, D_MODEL), tok.dtype),
        compiler_params=_cparams(("arbitrary",)), name="moe_dispatch",
    )(src, n_used, tok)


def _combine_kernel(d0_ref, d1_ref, y_hbm, x_ref, gate_ref, g2_ref, gn_ref, o_ref, buf_ref, sem, *,
                    tc, n_lat_tiles, tiles_per_batch, nb, final):
    i = pl.program_id(0)

    def issue(r, carry):
        pltpu.make_async_copy(y_hbm.at[pl.ds(d0_ref[i * tc + r], 1), :], buf_ref.at[0, pl.ds(r, 1), :], sem).start()
        pltpu.make_async_copy(y_hbm.at[pl.ds(d1_ref[i * tc + r], 1), :], buf_ref.at[1, pl.ds(r, 1), :], sem).start()
        return carry

    lax.fori_loop(0, tc, issue, 0, unroll=8)
    pltpu.make_async_copy(y_hbm.at[pl.ds(0, tc), :], buf_ref.at[0], sem).wait()
    pltpu.make_async_copy(y_hbm.at[pl.ds(0, tc), :], buf_ref.at[1], sem).wait()
    seg = _segment(i, n_lat_tiles, tiles_per_batch, nb)
    gates = gate_ref[...]
    f = gates[:, 0:1] * buf_ref[0] + gates[:, 1:2] * buf_ref[1]
    xo = x_ref[...] + g2_ref[pl.ds(seg, 1), :] * f
    if final:
        xo = _rms_rows(xo) * gn_ref[...]
    o_ref[...] = xo


def _combine(dest0, dest1, y, xs, gates, mod_l, gn, *, tc, n_tiles, n_lat_tiles, tiles_per_batch, nb, final):
    out_rows = n_tiles * tc
    return pl.pallas_call(
        functools.partial(_combine_kernel, tc=tc, n_lat_tiles=n_lat_tiles, tiles_per_batch=tiles_per_batch,
                          nb=nb, final=final),
        grid_spec=pltpu.PrefetchScalarGridSpec(
            num_scalar_prefetch=2, grid=(n_tiles,),
            in_specs=[pl.BlockSpec(memory_space=pl.ANY),
                      pl.BlockSpec((tc, D_MODEL), lambda i, a, b: (i, 0)),
                      pl.BlockSpec((tc, 2), lambda i, a, b: (i, 0)),
                      _mod_spec(M_G2, 3),
                      pl.BlockSpec((1, D_MODEL), lambda i, a, b: (0, 0))],
            out_specs=pl.BlockSpec((tc, D_MODEL), lambda i, a, b: (i, 0)),
            scratch_shapes=[pltpu.VMEM((2, tc, D_MODEL), F32), pltpu.SemaphoreType.DMA(())]),
        out_shape=jax.ShapeDtypeStruct((out_rows, D_MODEL), F32),
        compiler_params=_cparams(("arbitrary",)), name="moe_combine",
    )(dest0, dest1, y, xs, gates, mod_l, gn)


def _moe(tok, xs, mod_l, w_router, w1, w3, w2, gn_final, *, n_rows, tm, n_lat_tiles, nb, s_len, final):
    bs = ROW_TILE
    ids, gates, counts = _router(tok, w_router.T.astype(BF16), n_rows=n_rows, tr=tm)
    counts = counts[:, 0]
    padded = (counts + bs - 1) // bs * bs
    pad_ends = jnp.cumsum(padded)
    starts = pad_ends - padded
    dest0 = starts[ids[0]] + ids[2]
    dest1 = starts[ids[1]] + ids[3]
    n_blocks = (2 * n_rows + bs - 1) // bs + N_EXPERTS
    block_start = jnp.arange(n_blocks, dtype=I32) * bs
    block_expert = jnp.minimum(jnp.sum((pad_ends[None, :] <= block_start[:, None]).astype(I32), axis=1),
                               N_EXPERTS - 1)
    n_used = (pad_ends[-1:] // bs).astype(I32)
    tid = jnp.arange(n_rows, dtype=I32)
    src = jnp.zeros((n_blocks * bs,), I32).at[dest0].set(tid).at[dest1].set(tid)
    buf = _dispatch(tok, src, n_used, bs=bs, n_blocks=n_blocks)
    y = _ffn(buf, w1, w3, w2, block_expert, n_used, tm=bs, n_blocks=n_blocks)
    tc = 256
    return _combine(dest0, dest1, y, xs, gates[0:2].T, mod_l, gn_final, tc=tc, n_tiles=n_rows // tc,
                    n_lat_tiles=n_lat_tiles * (tm // tc), tiles_per_batch=s_len // tc, nb=nb, final=final)


def kernel(x, c, ctx, c_ctx, w_mod, b_mod, norm_mix, norm_ffn, norm_final, w_in, attn_sink, ret_decay_fwd,
           ret_decay_bwd, diff_lambda, diff_norm, w_branch, w_out, ffn_w1, ffn_w3, ffn_w2, moe_router, moe_w1,
           moe_w3, moe_w2):
    nb, s_len, d = x.shape
    c_len = ctx.shape[1]
    depth = w_in.shape[0]
    assert d == D_MODEL and nb < MOD_ROWS and s_len % 256 == 0 and s_len % c_len == 0 and c_len % RET_CHUNK == 0
    n_lat, n_ctx = nb * s_len, nb * c_len
    tm = ROW_TILE
    assert s_len % tm == 0 and n_ctx % tm == 0
    n_lat_tiles, n_all_tiles, tiles_per_batch = n_lat // tm, (n_lat + n_ctx) // tm, s_len // tm
    tile_kw = dict(tm=tm, n_lat_tiles=n_lat_tiles, tiles_per_batch=tiles_per_batch, nb=nb)

    cc = jnp.zeros((MOD_ROWS, d), F32).at[:nb].set(c).at[nb].set(c_ctx)
    mod = _modulation(cc, w_mod, b_mod)
    tabs = _rope_tables(s_len, tm)
    xs = jnp.concatenate([x.reshape(n_lat, d), ctx.reshape(n_ctx, d)], axis=0)
    one_block = jnp.zeros((n_all_tiles,), I32)
    row = lambda v: v.reshape(1, -1).astype(F32)

    for l in range(depth):
        ctx_out = l < depth - 1
        lam_init = 0.8 - 0.6 * math.exp(-0.3 * l)
        n_tiles = n_all_tiles if ctx_out else n_lat_tiles
        proj = _project(xs, row(norm_mix[l]), mod[l], _proj_weight(w_in[l]), tabs, n_tiles=n_all_tiles, **tile_kw)

        sink_tab = jnp.broadcast_to(attn_sink[l].astype(F32)[:, None], (N_HEADS_A, LANES))
        dec_tab = jnp.broadcast_to(jnp.stack([ret_decay_fwd[l], ret_decay_bwd[l]]).astype(F32)[:, :, None],
                                   (2, N_HEADS_B, LANES))
        lam_p, gain = diff_lambda[l].astype(F32), row(diff_norm[l])
        kw = dict(nb=nb, s_len=s_len, c_len=c_len)
        zero_state = jnp.zeros((nb, N_HEADS_B, 2, LANES, LANES), F32)
        oa, ob, oc = (jnp.zeros((n_lat + n_ctx, BRANCH_W), BF16) for _ in range(N_BRANCH))
        ob, state = _retention(proj, dec_tab, zero_state, ob, latent=False, **kw)
        ob, _ = _retention(proj, dec_tab, state, ob, latent=True, **kw)
        if ctx_out:
            oa = _gqa(proj, sink_tab, oa, local=False, **kw)
            oc = _diff_attention(proj, lam_p, gain, oc, lam_init=lam_init, latent=False, **kw)
        oa = _gqa(proj, sink_tab, oa, local=True, **kw)
        oc = _diff_attention(proj, lam_p, gain, oc, lam_init=lam_init, latent=True, **kw)

        is_moe = l % 2 == 1
        xs, tok = _merge(oa, ob, oc, proj, w_branch[l].astype(BF16), w_out[l].astype(BF16), xs, mod[l],
                         row(norm_ffn[l]), n_tiles=n_tiles, tok_dtype=F32 if is_moe else BF16, **tile_kw)
        final = l == depth - 1
        if is_moe:
            e = l // 2
            xs = _moe(tok, xs, mod[l], moe_router[e], moe_w1[e].astype(BF16), moe_w3[e].astype(BF16),
                      moe_w2[e].astype(BF16), row(norm_final), n_rows=n_tiles * tm, tm=tm,
                      n_lat_tiles=n_lat_tiles, nb=nb, s_len=s_len, final=final)
        else:
            e = l // 2
            xs = _ffn(tok, ffn_w1[e:e + 1].astype(BF16), ffn_w3[e:e + 1].astype(BF16), ffn_w2[e:e + 1].astype(BF16),
                      one_block, jnp.full((1,), n_tiles, I32), n_blocks=n_tiles, xs=xs, mod_l=mod[l], **tile_kw)
            assert not final
    return xs[:n_lat].reshape(nb, s_len, d)
```

```python
import functools
import math

import jax
import jax.numpy as jnp
from jax import lax
from jax.experimental import pallas as pl
from jax.experimental.pallas import tpu as pltpu

F32 = jnp.float32
BF16 = jnp.bfloat16
I32 = jnp.int32

D_MODEL = 1024
GRID_W = 64
HEAD_DIM = 64
N_HEADS_A = 8
N_KV_A = 2
REP_A = N_HEADS_A // N_KV_A
WINDOW = 128
ATTN_BLOCK = 128
N_HEADS_B = 4
RET_CHUNK = 256
N_HEADS_C = 4
N_BRANCH = 3
BRANCH_W = 512
D_FF = 2816
N_EXPERTS = 8
ROPE_BASE = 10000.0
ROPE_FREQS = HEAD_DIM // 4
NORM_EPS = 1e-6
NEG_INF = -1e30
LOG2E = math.log2(math.e)

LANES = 128
MOD_ROWS = 16
G_QA, G_KA, G_QB, G_KB, G_QD, G_KD = 0, 4, 6, 8, 10, 14
N_ROPE_GROUPS = 18
G_VA, G_VB, G_GB, G_VD, G_GL = 18, 20, 24, 28, 32
PROJ_W = 56 * LANES
M_SH1, M_SC1, M_G1, M_SH2, M_SC2, M_G2 = range(6)

FF_TILE = 1408
ROW_TILE = 512
PROJ_ROW_TILE = 1024
DIFF_Q_TILE = 512
DIFF_K_TILE = 256
VMEM_LIMIT = 56 * 1024 * 1024


def _cparams(sem):
    return pltpu.CompilerParams(dimension_semantics=sem, vmem_limit_bytes=VMEM_LIMIT)


def _sigmoid(x):
    return 1.0 / (1.0 + jnp.exp(-x))


def _nt_dot(a, b):
    return lax.dot_general(a, b, (((1,), (1,)), ((), ())), preferred_element_type=F32)


def _rms_rows(x):
    return x * lax.rsqrt(jnp.mean(x * x, axis=-1, keepdims=True) + NORM_EPS)


def _mod_kernel(c_ref, w_ref, b_ref, o_ref):
    c = c_ref[...]
    s = c * _sigmoid(c)
    o_ref[0] = jnp.dot(s, w_ref[0], preferred_element_type=F32,
                       precision=lax.Precision.HIGHEST) + b_ref[0]


def _modulation(cc, w_mod, b_mod):
    depth, d, n = w_mod.shape
    tn = 1536
    return pl.pallas_call(
        _mod_kernel,
        grid=(depth, n // tn),
        in_specs=[pl.BlockSpec((MOD_ROWS, d), lambda l, j: (0, 0)),
                  pl.BlockSpec((1, d, tn), lambda l, j: (l, 0, j)),
                  pl.BlockSpec((1, 1, tn), lambda l, j: (l, 0, j))],
        out_specs=pl.BlockSpec((1, MOD_ROWS, tn), lambda l, j: (l, 0, j)),
        out_shape=jax.ShapeDtypeStruct((depth, MOD_ROWS, n), F32),
        compiler_params=_cparams(("arbitrary", "arbitrary")),
        name="modulation",
    )(cc, w_mod, b_mod.reshape(depth, 1, n))


def _mod_spec(chunk, nargs):
    if nargs == 1:
        return pl.BlockSpec((MOD_ROWS, D_MODEL), lambda i: (0, chunk))
    if nargs == 2:
        return pl.BlockSpec((MOD_ROWS, D_MODEL), lambda i, j: (0, chunk))
    return pl.BlockSpec((MOD_ROWS, D_MODEL), lambda i, j, *_: (0, chunk))


def _segment(i, n_lat_tiles, tiles_per_batch, nb):
    return jnp.where(i < n_lat_tiles, i // tiles_per_batch, nb)


def _proj_kernel(x_ref, gn_ref, sh_ref, sc_ref, w_ref, cos_ref, sa_ref, sb_ref, o_ref, h_ref, *,
                 n_lat_tiles, tiles_per_batch, nb, groups):
    i = pl.program_id(0)
    j = pl.program_id(1)

    @pl.when(j == 0)
    def _():
        seg = _segment(i, n_lat_tiles, tiles_per_batch, nb)
        sc = sc_ref[pl.ds(seg, 1), :]
        sh = sh_ref[pl.ds(seg, 1), :]
        h = (_rms_rows(x_ref[...]) * gn_ref[...]) * (1.0 + sc) + sh
        h_ref[...] = h.astype(BF16)

    acc = jnp.dot(h_ref[...], w_ref[...], preferred_element_type=F32)
    for g in range(groups):
        cols = slice(g * LANES, (g + 1) * LANES)
        a = acc[:, cols]
        gi = j * groups + g

        @pl.when(gi < N_ROPE_GROUPS)
        def _():
            y = (a * cos_ref[...] + pltpu.roll(a, LANES - 16, 1) * sa_ref[...]
                 + pltpu.roll(a, 16, 1) * sb_ref[...])
            o_ref[:, cols] = y.astype(BF16)

        @pl.when(gi >= N_ROPE_GROUPS)
        def _():
            o_ref[:, cols] = a.astype(BF16)


def _project(xs, gn, mod_l, w, tabs, *, tm, n_tiles, n_lat_tiles, tiles_per_batch, nb):
    tn = 1024
    groups = tn // LANES
    rope_map = lambda i, j: (jnp.where(i < n_lat_tiles, i % tiles_per_batch, tiles_per_batch), 0)
    kern = functools.partial(_proj_kernel, n_lat_tiles=n_lat_tiles, tiles_per_batch=tiles_per_batch,
                             nb=nb, groups=groups)
    return pl.pallas_call(
        kern,
        grid=(n_tiles, PROJ_W // tn),
        in_specs=[pl.BlockSpec((tm, D_MODEL), lambda i, j: (i, 0)),
                  pl.BlockSpec((1, D_MODEL), lambda i, j: (0, 0)),
                  _mod_spec(M_SH1, 2), _mod_spec(M_SC1, 2),
                  pl.BlockSpec((D_MODEL, tn), lambda i, j: (0, j)),
                  pl.BlockSpec((tm, LANES), rope_map),
                  pl.BlockSpec((tm, LANES), rope_map),
                  pl.BlockSpec((tm, LANES), rope_map)],
        out_specs=pl.BlockSpec((tm, tn), lambda i, j: (i, j)),
        out_shape=jax.ShapeDtypeStruct((xs.shape[0], PROJ_W), BF16),
        scratch_shapes=[pltpu.VMEM((tm, D_MODEL), BF16)],
        compiler_params=_cparams(("arbitrary", "arbitrary")),
        name="in_proj",
    )(xs, gn, mod_l, mod_l, w, *tabs)


def _rope_tables(s, tm):
    rows = s // GRID_W
    row = jnp.repeat(jnp.arange(rows, dtype=F32), GRID_W)
    col = jnp.tile(jnp.arange(GRID_W, dtype=F32), rows)
    inv = 1.0 / (ROPE_BASE ** (jnp.arange(ROPE_FREQS, dtype=F32) / ROPE_FREQS))
    ar, ac = row[:, None] * inv, col[:, None] * inv
    z = jnp.zeros_like(ar)
    cos = jnp.concatenate([jnp.cos(ar), jnp.cos(ar), jnp.cos(ac), jnp.cos(ac)], axis=1)
    sa = jnp.concatenate([-jnp.sin(ar), z, -jnp.sin(ac), z], axis=1)
    sb = jnp.concatenate([z, jnp.sin(ar), z, jnp.sin(ac)], axis=1)
    ident = lambda t, v: jnp.concatenate([jnp.tile(t, (1, 2)), jnp.full((tm, LANES), v, F32)], axis=0)
    return ident(cos, 1.0), ident(sa, 0.0), ident(sb, 0.0)


def _proj_weight(w):
    sl = lambda a, b: w[:, a:b]
    sc = HEAD_DIM ** -0.5
    dup = lambda a: jnp.concatenate([sl(a, a + 64), sl(a, a + 64), sl(a + 64, a + 128), sl(a + 64, a + 128)], axis=1)
    parts = [sl(0, 512) * (sc * LOG2E), dup(512), sl(768, 1024), sl(1024, 1280) * sc, sl(2304, 2816) * (sc * LOG2E),
             sl(2816, 3328), dup(640), sl(1280, 1792), sl(1792, 2304), sl(3328, 3840), sl(3840, 6912)]
    return jnp.concatenate(parts, axis=1).astype(BF16)


def _gqa_kernel(*refs, local, s_len, tq):
    if local:
        (q_ref, kp_ref, kc_ref, kn_ref, kx_ref, vp_ref, vc_ref, vn_ref, vx_ref, sink_ref, _, o_ref) = refs
    else:
        q_ref, kx_ref, vx_ref, sink_ref, _, o_ref = refs
    lo = lax.broadcasted_iota(I32, (tq, LANES), 1) < HEAD_DIM
    if local:
        nk = 3 * tq + kx_ref.shape[0]
        qi = lax.broadcasted_iota(I32, (tq, nk), 0)
        kj = lax.broadcasted_iota(I32, (tq, nk), 1)
        rel = kj - qi
        kpos = pl.program_id(1) * tq - WINDOW + kj
        bad = ((rel < 0) | (rel > 2 * WINDOW) | (kpos < 0) | (kpos >= s_len)) & (kj < 3 * tq)
    for g in range(N_KV_A):
        cols = slice(g * LANES, (g + 1) * LANES)
        if local:
            kcat = jnp.concatenate([kp_ref[:, cols], kc_ref[:, cols], kn_ref[:, cols], kx_ref[:, cols]], axis=0)
            vcat = jnp.concatenate([vp_ref[:, cols], vc_ref[:, cols], vn_ref[:, cols], vx_ref[:, cols]], axis=0)
        else:
            kcat, vcat = kx_ref[:, cols], vx_ref[:, cols]
        qs = []
        for r in range(REP_A):
            h = g * REP_A + r
            qp = q_ref[:, (h // 2) * LANES:(h // 2 + 1) * LANES]
            qs.append(jnp.where(lo if h % 2 == 0 else jnp.logical_not(lo), qp, jnp.zeros_like(qp)))
        s = _nt_dot(jnp.concatenate(qs, axis=0), kcat)
        ps, dens = [], []
        for r in range(REP_A):
            h = g * REP_A + r
            sr = s[r * tq:(r + 1) * tq]
            if local:
                sr = jnp.where(bad, NEG_INF, sr)
            sk = sink_ref[h:h + 1, 0:1] * LOG2E
            m = jnp.maximum(jnp.max(sr, axis=-1, keepdims=True), sk)
            p = jnp.exp2(sr - m)
            dens.append(jnp.sum(p, axis=-1, keepdims=True) + jnp.exp2(sk - m))
            ps.append(p.astype(BF16))
        o = jnp.dot(jnp.concatenate(ps, axis=0), vcat, preferred_element_type=F32)
        outs = [o[r * tq:(r + 1) * tq] / dens[r] for r in range(REP_A)]
        for pr in range(REP_A // 2):
            pair = jnp.where(lo, outs[2 * pr], outs[2 * pr + 1])
            c0 = (g * (REP_A // 2) + pr) * LANES
            o_ref[:, c0:c0 + LANES] = pair.astype(BF16)


def _gqa(proj, sink_tab, o_prev, *, nb, s_len, c_len, local):
    n_rows = proj.shape[0]
    cblk = nb * s_len // c_len
    kcol, vcol = G_KA // 2, G_VA // 2
    out_shape = jax.ShapeDtypeStruct((n_rows, N_HEADS_A * HEAD_DIM), BF16)
    if local:
        tq = ATTN_BLOCK
        nq = s_len // tq
        row = lambda b, i: b * nq + i
        specs = [pl.BlockSpec((tq, 4 * LANES), lambda b, i: (row(b, i), 0))]
        for col in (kcol, vcol):
            specs += [pl.BlockSpec((tq, 2 * LANES), lambda b, i, col=col: (row(b, jnp.maximum(i - 1, 0)), col)),
                      pl.BlockSpec((tq, 2 * LANES), lambda b, i, col=col: (row(b, i), col)),
                      pl.BlockSpec((tq, 2 * LANES), lambda b, i, col=col: (row(b, jnp.minimum(i + 1, nq - 1)), col)),
                      pl.BlockSpec((c_len, 2 * LANES), lambda b, i, col=col: (cblk + b, col))]
        specs += [pl.BlockSpec((N_HEADS_A, LANES), lambda b, i: (0, 0)),
                  pl.BlockSpec(memory_space=pl.ANY)]
        args = [proj] + [proj] * 8 + [sink_tab, o_prev]
        return pl.pallas_call(
            functools.partial(_gqa_kernel, local=True, s_len=s_len, tq=tq),
            grid=(nb, nq), in_specs=specs,
            out_specs=pl.BlockSpec((tq, 4 * LANES), lambda b, i: (row(b, i), 0)),
            out_shape=out_shape, input_output_aliases={10: 0},
            compiler_params=_cparams(("arbitrary", "arbitrary")), name="gqa_window",
        )(*args)
    specs = [pl.BlockSpec((c_len, 4 * LANES), lambda b: (cblk + b, 0)),
             pl.BlockSpec((c_len, 2 * LANES), lambda b: (cblk + b, kcol)),
             pl.BlockSpec((c_len, 2 * LANES), lambda b: (cblk + b, vcol)),
             pl.BlockSpec((N_HEADS_A, LANES), lambda b: (0, 0)),
             pl.BlockSpec(memory_space=pl.ANY)]
    return pl.pallas_call(
        functools.partial(_gqa_kernel, local=False, s_len=s_len, tq=c_len),
        grid=(nb,), in_specs=specs,
        out_specs=pl.BlockSpec((c_len, 4 * LANES), lambda b: (cblk + b, 0)),
        out_shape=out_shape, input_output_aliases={4: 0},
        compiler_params=_cparams(("arbitrary",)), name="gqa_context",
    )(proj, proj, proj, sink_tab, o_prev)


def _ret_kernel(q_ref, k_ref, v_ref, g_ref, dec_ref, sin_ref, _, o_ref, sout_ref, of_ref, ob_ref, *, n_rows):
    h = pl.program_id(1)
    c = RET_CHUNK
    nch = n_rows // c
    lane = lax.broadcasted_iota(I32, (c, LANES), 1)
    row = lax.broadcasted_iota(I32, (c, LANES), 0)
    qmask = (lane // HEAD_DIM) == (h % 2)

    def log_sigmoid(d):
        x = dec_ref[d, pl.ds(h, 1), :]
        return jnp.minimum(x, 0.0) - jnp.log(1.0 + jnp.exp(-jnp.abs(x)))

    lg_f, lg_b = log_sigmoid(0), log_sigmoid(1)
    diff = lax.broadcasted_iota(I32, (c, c), 0) - lax.broadcasted_iota(I32, (c, c), 1)
    rowf = row.astype(F32)
    dmat_f = jnp.where(diff >= 0, jnp.exp(jnp.maximum(diff, 0).astype(F32) * lg_f[:, 0:1]), 0.0)
    dmat_b = jnp.where(diff <= 0, jnp.exp(jnp.maximum(-diff, 0).astype(F32) * lg_b[:, 0:1]), 0.0)
    qdec_f = jnp.exp((rowf + 1.0) * lg_f)
    kdec_f = jnp.exp((c - 1.0 - rowf) * lg_f)
    qdec_b = jnp.exp((c - rowf) * lg_b)
    kdec_b = jnp.exp(rowf * lg_b)
    cdec_f = jnp.exp(c * lg_f)
    cdec_b = jnp.exp(c * lg_b)

    def chunk(n, state, dmat, qdec, kdec, cdec, dst_ref):
        rows = pl.ds(pl.multiple_of(n * c, c), c)
        q = q_ref[rows, :]
        q = jnp.where(qmask, q, jnp.zeros_like(q))
        k = k_ref[rows, :]
        v = v_ref[rows, :]
        sc = (_nt_dot(q, k) * dmat).astype(BF16)
        intra = jnp.dot(sc, v, preferred_element_type=F32)
        inter = jnp.dot((q.astype(F32) * qdec).astype(BF16), state.astype(BF16), preferred_element_type=F32)
        dst_ref[rows, :] = intra + inter
        kd = (k.astype(F32) * kdec).T.astype(BF16)
        return state * cdec + jnp.dot(kd, v, preferred_element_type=F32)

    def body(n, carry):
        sf, sb = carry
        sf = chunk(n, sf, dmat_f, qdec_f, kdec_f, cdec_f, of_ref)
        sb = chunk(nch - 1 - n, sb, dmat_b, qdec_b, kdec_b, cdec_b, ob_ref)
        return sf, sb

    sf, sb = lax.fori_loop(0, nch, body, (sin_ref[0, 0, 0], sin_ref[0, 0, 1]), unroll=2)
    sout_ref[0, 0, 0] = sf
    sout_ref[0, 0, 1] = sb
    o = of_ref[...] + ob_ref[...]
    gate = g_ref[...].astype(F32)
    o_ref[...] = (_rms_rows(o) * (gate * _sigmoid(gate))).astype(BF16)


def _retention(proj, dec_tab, s_in, o_prev, *, nb, s_len, c_len, latent):
    n_rows = s_len if latent else c_len
    base = 0 if latent else nb * s_len // c_len
    rb = lambda b: base + b
    col = lambda g0, per_pair: (lambda b, h: (rb(b), g0 + (h // 2 if per_pair else h)))
    state_spec = pl.BlockSpec((1, 1, 2, LANES, LANES), lambda b, h: (b, h, 0, 0, 0))
    specs = [pl.BlockSpec((n_rows, LANES), col(G_QB, True)),
             pl.BlockSpec((n_rows, LANES), col(G_KB, True)),
             pl.BlockSpec((n_rows, LANES), col(G_VB, False)),
             pl.BlockSpec((n_rows, LANES), col(G_GB, False)),
             pl.BlockSpec((2, N_HEADS_B, LANES), lambda b, h: (0, 0, 0)),
             state_spec, pl.BlockSpec(memory_space=pl.ANY)]
    args = [proj, proj, proj, proj, dec_tab, s_in, o_prev]
    aliases = {6: 0}
    return pl.pallas_call(
        functools.partial(_ret_kernel, n_rows=n_rows),
        grid=(nb, N_HEADS_B), in_specs=specs,
        out_specs=[pl.BlockSpec((n_rows, LANES), lambda b, h: (rb(b), h)), state_spec],
        out_shape=[jax.ShapeDtypeStruct((proj.shape[0], N_HEADS_B * LANES), BF16),
                   jax.ShapeDtypeStruct(s_in.shape, F32)],
        scratch_shapes=[pltpu.VMEM((n_rows, LANES), F32), pltpu.VMEM((n_rows, LANES), F32)],
        input_output_aliases=aliases,
        compiler_params=_cparams(("arbitrary", "arbitrary")),
        name="retention_latent" if latent else "retention_context",
    )(*args)


def _diff_kernel(*refs, latent, lam_init):
    if latent:
        q_ref, kl_ref, kx_ref, vl_ref, vx_ref, lam_ref, gain_ref, _, o_ref, k_scr, v_scr = refs
    else:
        q_ref, kx_ref, vx_ref, lam_ref, gain_ref, _, o_ref, k_scr, v_scr = refs
    nl = kl_ref.shape[0] if latent else 0

    def fill():
        if latent:
            k_scr[0:nl, :] = kl_ref[...]
            v_scr[0:nl, 0:LANES] = vl_ref[...]
        k_scr[nl:, :] = kx_ref[...]
        v_scr[nl:, 0:LANES] = vx_ref[...]
        v_scr[:, LANES:] = jnp.ones((v_scr.shape[0], LANES), BF16)

    if latent:
        pl.when(pl.program_id(2) == 0)(fill)
    else:
        fill()
    lp = lam_ref[...]
    lam = (jnp.exp(jnp.sum(lp[0:1] * lp[1:2], axis=-1, keepdims=True))
           - jnp.exp(jnp.sum(lp[2:3] * lp[3:4], axis=-1, keepdims=True)) + lam_init)
    q = q_ref[...]
    lo = lax.broadcasted_iota(I32, q.shape, 1) < HEAD_DIM
    zero = jnp.zeros_like(q)
    n_keys = k_scr.shape[0]
    outs = []
    for comp in range(2):
        qc = jnp.where(lo, q, zero) if comp == 0 else jnp.where(lo, zero, q)
        m = acc = None
        for start in range(0, n_keys, DIFF_K_TILE):
            size = min(DIFF_K_TILE, n_keys - start)
            s = _nt_dot(qc, k_scr[start:start + size, :])
            m_t = jnp.max(s, axis=-1, keepdims=True)
            m_new = m_t if m is None else jnp.maximum(m, m_t)
            p = jnp.exp2(s - m_new).astype(BF16)
            pv = jnp.dot(p, v_scr[start:start + size, :], preferred_element_type=F32)
            acc = pv if acc is None else jnp.exp2(m - m_new) * acc + pv
            m = m_new
        outs.append(acc[:, :LANES] / acc[:, LANES:])
    o = outs[0] - lam * outs[1]
    o_ref[...] = (_rms_rows(o) * gain_ref[...] * (1.0 - lam_init)).astype(BF16)


def _diff_attention(proj, lam_p, gain, o_prev, *, nb, s_len, c_len, lam_init, latent):
    cblk = nb * s_len // c_len
    out_shape = jax.ShapeDtypeStruct((proj.shape[0], N_HEADS_C * LANES), BF16)
    small = [pl.BlockSpec((4, HEAD_DIM), lambda *_: (0, 0)), pl.BlockSpec((1, LANES), lambda *_: (0, 0))]
    if latent:
        tq = min(DIFF_Q_TILE, s_len)
        nq = s_len // tq
        specs = [pl.BlockSpec((tq, LANES), lambda b, h, i: (b * nq + i, G_QD + h)),
                 pl.BlockSpec((s_len, LANES), lambda b, h, i: (b, G_KD + h)),
                 pl.BlockSpec((c_len, LANES), lambda b, h, i: (cblk + b, G_KD + h)),
                 pl.BlockSpec((s_len, LANES), lambda b, h, i: (b, G_VD + h)),
                 pl.BlockSpec((c_len, LANES), lambda b, h, i: (cblk + b, G_VD + h))] + small
        specs.append(pl.BlockSpec(memory_space=pl.ANY))
        return pl.pallas_call(
            functools.partial(_diff_kernel, latent=True, lam_init=lam_init),
            grid=(nb, N_HEADS_C, nq), in_specs=specs,
            out_specs=pl.BlockSpec((tq, LANES), lambda b, h, i: (b * nq + i, h)),
            out_shape=out_shape, input_output_aliases={7: 0},
            scratch_shapes=[pltpu.VMEM((s_len + c_len, LANES), BF16), pltpu.VMEM((s_len + c_len, 2 * LANES), BF16)],
            compiler_params=_cparams(("arbitrary", "arbitrary", "arbitrary")), name="diff_latent",
        )(proj, proj, proj, proj, proj, lam_p, gain, o_prev)
    specs = [pl.BlockSpec((c_len, LANES), lambda b, h: (cblk + b, G_QD + h)),
             pl.BlockSpec((c_len, LANES), lambda b, h: (cblk + b, G_KD + h)),
             pl.BlockSpec((c_len, LANES), lambda b, h: (cblk + b, G_VD + h))] + small
    specs.append(pl.BlockSpec(memory_space=pl.ANY))
    return pl.pallas_call(
        functools.partial(_diff_kernel, latent=False, lam_init=lam_init),
        grid=(nb, N_HEADS_C), in_specs=specs,
        out_specs=pl.BlockSpec((c_len, LANES), lambda b, h: (cblk + b, h)),
        out_shape=out_shape, input_output_aliases={5: 0},
        scratch_shapes=[pltpu.VMEM((c_len, LANES), BF16), pltpu.VMEM((c_len, 2 * LANES), BF16)],
        compiler_params=_cparams(("arbitrary", "arbitrary")), name="diff_context",
    )(proj, proj, proj, lam_p, gain, o_prev)


def _merge_kernel(oa_ref, ob_ref, oc_ref, ga_ref, gb_ref, gc_ref, wb_ref, wo_ref, x_ref, g1_ref, sh_ref, sc_ref,
                  gn_ref, xo_ref, tok_ref, *, n_lat_tiles, tiles_per_batch, nb):
    i = pl.program_id(0)
    seg = _segment(i, n_lat_tiles, tiles_per_batch, nb)
    y = None
    for k, (o_ref, gl_ref) in enumerate(((oa_ref, ga_ref), (ob_ref, gb_ref), (oc_ref, gc_ref))):
        t = _sigmoid(gl_ref[...].astype(F32)) * jnp.dot(o_ref[...], wb_ref[k], preferred_element_type=F32)
        y = t if y is None else y + t
    z = jnp.dot(y.astype(BF16), wo_ref[...], preferred_element_type=F32)
    xn = x_ref[...] + g1_ref[pl.ds(seg, 1), :] * z
    xo_ref[...] = xn
    tok = (_rms_rows(xn) * gn_ref[...]) * (1.0 + sc_ref[pl.ds(seg, 1), :]) + sh_ref[pl.ds(seg, 1), :]
    tok_ref[...] = tok.astype(tok_ref.dtype)


def _merge(oa, ob, oc, proj, wb, wo, xs, mod_l, gn, *, tm, n_tiles, n_lat_tiles, tiles_per_batch, nb, tok_dtype):
    row = lambda w: pl.BlockSpec((tm, w), lambda i: (i, 0))
    gl = lambda k: pl.BlockSpec((tm, D_MODEL), lambda i: (i, G_GL // 8 + k))
    return pl.pallas_call(
        functools.partial(_merge_kernel, n_lat_tiles=n_lat_tiles, tiles_per_batch=tiles_per_batch, nb=nb),
        grid=(n_tiles,),
        in_specs=[row(BRANCH_W), row(BRANCH_W), row(BRANCH_W), gl(0), gl(1), gl(2),
                  pl.BlockSpec((N_BRANCH, BRANCH_W, D_MODEL), lambda i: (0, 0, 0)),
                  pl.BlockSpec((D_MODEL, D_MODEL), lambda i: (0, 0)),
                  row(D_MODEL), _mod_spec(M_G1, 1), _mod_spec(M_SH2, 1), _mod_spec(M_SC2, 1),
                  pl.BlockSpec((1, D_MODEL), lambda i: (0, 0))],
        out_specs=[row(D_MODEL), row(D_MODEL)],
        out_shape=[jax.ShapeDtypeStruct(xs.shape, F32), jax.ShapeDtypeStruct((n_tiles * tm, D_MODEL), tok_dtype)],
        input_output_aliases={8: 0},
        compiler_params=_cparams(("arbitrary",)), name="merge",
    )(oa, ob, oc, proj, proj, proj, wb, wo, xs, mod_l, mod_l, mod_l, gn)


def _ffn_kernel(be_ref, nu_ref, *refs, dense, n_lat_tiles, tiles_per_batch, nb):
    if dense:
        t_ref, w1_ref, w3_ref, w2_ref, x_ref, g2_ref, o_ref, acc_ref = refs
    else:
        t_ref, w1_ref, w3_ref, w2_ref, o_ref, acc_ref = refs
    i = pl.program_id(0)
    f = pl.program_id(1)
    last = pl.num_programs(1) - 1
    used = i < nu_ref[0]

    @pl.when(used)
    def _():
        t = t_ref[...].astype(BF16)
        h1 = jnp.dot(t, w1_ref[0], preferred_element_type=F32)
        h3 = jnp.dot(t, w3_ref[0], preferred_element_type=F32)
        a = ((h1 * _sigmoid(h1)) * h3).astype(BF16)
        part = jnp.dot(a, w2_ref[0], preferred_element_type=F32)

        @pl.when(f == 0)
        def _():
            acc_ref[...] = part

        @pl.when(f > 0)
        def _():
            acc_ref[...] += part

    @pl.when(f == last)
    def _():
        if dense:
            seg = _segment(i, n_lat_tiles, tiles_per_batch, nb)
            o_ref[...] = x_ref[...] + g2_ref[pl.ds(seg, 1), :] * acc_ref[...]
        else:
            o_ref[...] = jnp.where(used, acc_ref[...], 0.0)


def _ffn(tok, w1, w3, w2, block_expert, n_used, *, tm, n_blocks, xs=None, mod_l=None,
         n_lat_tiles=0, tiles_per_batch=1, nb=0):
    dense = xs is not None
    nf = D_FF // FF_TILE
    specs = [pl.BlockSpec((tm, D_MODEL), lambda i, f, be, nu: (i, 0)),
             pl.BlockSpec((1, D_MODEL, FF_TILE), lambda i, f, be, nu: (be[i], 0, f)),
             pl.BlockSpec((1, D_MODEL, FF_TILE), lambda i, f, be, nu: (be[i], 0, f)),
             pl.BlockSpec((1, FF_TILE, D_MODEL), lambda i, f, be, nu: (be[i], f, 0))]
    args = [tok, w1, w3, w2]
    aliases = {}
    if dense:
        specs += [pl.BlockSpec((tm, D_MODEL), lambda i, f, be, nu: (i, 0)), _mod_spec(M_G2, 4)]
        args += [xs, mod_l]
        aliases = {6: 0}
        out_rows = xs.shape[0]
    else:
        out_rows = tok.shape[0]
    return pl.pallas_call(
        functools.partial(_ffn_kernel, dense=dense, n_lat_tiles=n_lat_tiles, tiles_per_batch=tiles_per_batch, nb=nb),
        grid_spec=pltpu.PrefetchScalarGridSpec(
            num_scalar_prefetch=2, grid=(n_blocks, nf), in_specs=specs,
            out_specs=pl.BlockSpec((tm, D_MODEL), lambda i, f, be, nu: (i, 0)),
            scratch_shapes=[pltpu.VMEM((tm, D_MODEL), F32)]),
        out_shape=jax.ShapeDtypeStruct((out_rows, D_MODEL), F32),
        input_output_aliases=aliases,
        compiler_params=_cparams(("arbitrary", "arbitrary")),
        name="ffn_dense" if dense else "ffn_experts",
    )(block_expert, n_used, *args)


def _router_kernel(tok_ref, wr_ref, ids_ref, gate_ref, cnt_ref, base_ref, *, tr):
    i = pl.program_id(0)

    @pl.when(i == 0)
    def _():
        base_ref[...] = jnp.zeros_like(base_ref)

    logits = _nt_dot(wr_ref[...], tok_ref[...].astype(BF16))
    eidx = lax.broadcasted_iota(I32, logits.shape, 0)
    m1 = jnp.max(logits, axis=0, keepdims=True)
    i1 = jnp.min(jnp.where(logits == m1, eidx, N_EXPERTS), axis=0, keepdims=True)
    rest = jnp.where(eidx == i1, -jnp.inf, logits)
    m2 = jnp.max(rest, axis=0, keepdims=True)
    i2 = jnp.min(jnp.where(rest == m2, eidx, N_EXPERTS), axis=0, keepdims=True)
    e = jnp.exp(m2 - m1)
    sel1 = eidx == i1
    sel2 = eidx == i2
    both = jnp.where(sel1 | sel2, 1.0, 0.0)
    before = lax.broadcasted_iota(I32, (tr, tr), 0) < lax.broadcasted_iota(I32, (tr, tr), 1)
    prefix = jnp.dot(both.astype(BF16), jnp.where(before, 1.0, 0.0).astype(BF16), preferred_element_type=F32)
    pos = base_ref[:, 0:1] + prefix
    r1 = jnp.sum(jnp.where(sel1, pos, 0.0), axis=0, keepdims=True)
    r2 = jnp.sum(jnp.where(sel2, pos, 0.0), axis=0, keepdims=True)
    ids_ref[0:1, :] = i1
    ids_ref[1:2, :] = i2
    ids_ref[2:3, :] = r1.astype(I32)
    ids_ref[3:4, :] = r2.astype(I32)
    ids_ref[4:8, :] = jnp.zeros((4, tr), I32)
    gate_ref[0:1, :] = 1.0 / (1.0 + e)
    gate_ref[1:2, :] = e / (1.0 + e)
    gate_ref[2:8, :] = jnp.zeros((6, tr), F32)
    base_ref[...] = base_ref[...] + jnp.sum(both, axis=1, keepdims=True)
    cnt_ref[...] = base_ref[...].astype(I32)


def _router(tok, wr, *, n_rows, tr):
    return pl.pallas_call(
        functools.partial(_router_kernel, tr=tr),
        grid=(n_rows // tr,),
        in_specs=[pl.BlockSpec((tr, D_MODEL), lambda i: (i, 0)),
                  pl.BlockSpec((N_EXPERTS, D_MODEL), lambda i: (0, 0))],
        out_specs=[pl.BlockSpec((8, tr), lambda i: (0, i)), pl.BlockSpec((8, tr), lambda i: (0, i)),
                   pl.BlockSpec((N_EXPERTS, LANES), lambda i: (0, 0))],
        out_shape=[jax.ShapeDtypeStruct((8, n_rows), I32), jax.ShapeDtypeStruct((8, n_rows), F32),
                   jax.ShapeDtypeStruct((N_EXPERTS, LANES), I32)],
        scratch_shapes=[pltpu.VMEM((N_EXPERTS, LANES), F32)],
        compiler_params=_cparams(("arbitrary",)), name="router",
    )(tok, wr)


def _scatter_kernel(d0_ref, d1_ref, tok_hbm, _, o_hbm, sem, *, tb):
    i = pl.program_id(0)

    def issue(r, carry):
        t = i * tb + r
        src = tok_hbm.at[pl.ds(t, 1), :]
        pltpu.make_async_copy(src, o_hbm.at[pl.ds(d0_ref[t], 1), :], sem).start()
        pltpu.make_async_copy(src, o_hbm.at[pl.ds(d1_ref[t], 1), :], sem).start()
        return carry

    lax.fori_loop(0, tb, issue, 0, unroll=8)
    pltpu.make_async_copy(tok_hbm.at[pl.ds(0, 2 * tb), :], o_hbm.at[pl.ds(0, 2 * tb), :], sem).wait()


def _dispatch(tok, dest0, dest1, *, n_rows, n_slots):
    tb = ROW_TILE
    return pl.pallas_call(
        functools.partial(_scatter_kernel, tb=tb),
        grid_spec=pltpu.PrefetchScalarGridSpec(
            num_scalar_prefetch=2, grid=(n_rows // tb,),
            in_specs=[pl.BlockSpec(memory_space=pl.ANY), pl.BlockSpec(memory_space=pl.ANY)],
            out_specs=pl.BlockSpec(memory_space=pl.ANY),
            scratch_shapes=[pltpu.SemaphoreType.DMA(())]),
        out_shape=jax.ShapeDtypeStruct((n_slots, D_MODEL), tok.dtype),
        input_output_aliases={3: 0},
        compiler_params=_cparams(("arbitrary",)), name="moe_dispatch",
    )(dest0, dest1, tok, jnp.zeros((n_slots, D_MODEL), tok.dtype))


def _combine_kernel(d0_ref, d1_ref, y_hbm, x_ref, gate_ref, g2_ref, gn_ref, o_ref, buf_ref, sem, *,
                    tc, n_lat_tiles, tiles_per_batch, nb, final):
    i = pl.program_id(0)

    def issue(r, carry):
        pltpu.make_async_copy(y_hbm.at[pl.ds(d0_ref[i * tc + r], 1), :], buf_ref.at[0, pl.ds(r, 1), :], sem).start()
        pltpu.make_async_copy(y_hbm.at[pl.ds(d1_ref[i * tc + r], 1), :], buf_ref.at[1, pl.ds(r, 1), :], sem).start()
        return carry

    lax.fori_loop(0, tc, issue, 0, unroll=8)
    pltpu.make_async_copy(y_hbm.at[pl.ds(0, tc), :], buf_ref.at[0], sem).wait()
    pltpu.make_async_copy(y_hbm.at[pl.ds(0, tc), :], buf_ref.at[1], sem).wait()
    seg = _segment(i, n_lat_tiles, tiles_per_batch, nb)
    gates = gate_ref[...]
    f = gates[:, 0:1] * buf_ref[0] + gates[:, 1:2] * buf_ref[1]
    xo = x_ref[...] + g2_ref[pl.ds(seg, 1), :] * f
    if final:
        xo = _rms_rows(xo) * gn_ref[...]
    o_ref[...] = xo


def _combine(dest0, dest1, y, xs, gates, mod_l, gn, *, tc, n_tiles, n_lat_tiles, tiles_per_batch, nb, final):
    out_rows = n_tiles * tc
    return pl.pallas_call(
        functools.partial(_combine_kernel, tc=tc, n_lat_tiles=n_lat_tiles, tiles_per_batch=tiles_per_batch,
                          nb=nb, final=final),
        grid_spec=pltpu.PrefetchScalarGridSpec(
            num_scalar_prefetch=2, grid=(n_tiles,),
            in_specs=[pl.BlockSpec(memory_space=pl.ANY),
                      pl.BlockSpec((tc, D_MODEL), lambda i, a, b: (i, 0)),
                      pl.BlockSpec((tc, 2), lambda i, a, b: (i, 0)),
                      _mod_spec(M_G2, 3),
                      pl.BlockSpec((1, D_MODEL), lambda i, a, b: (0, 0))],
            out_specs=pl.BlockSpec((tc, D_MODEL), lambda i, a, b: (i, 0)),
            scratch_shapes=[pltpu.VMEM((2, tc, D_MODEL), F32), pltpu.SemaphoreType.DMA(())]),
        out_shape=jax.ShapeDtypeStruct((out_rows, D_MODEL), F32),
        compiler_params=_cparams(("arbitrary",)), name="moe_combine",
    )(dest0, dest1, y, xs, gates, mod_l, gn)


def _moe(tok, xs, mod_l, w_router, w1, w3, w2, gn_final, *, n_rows, tm, n_lat_tiles, nb, s_len, final):
    bs = ROW_TILE
    ids, gates, counts = _router(tok, w_router.T.astype(BF16), n_rows=n_rows, tr=tm)
    counts = counts[:, 0]
    padded = (counts + bs - 1) // bs * bs
    pad_ends = jnp.cumsum(padded)
    starts = pad_ends - padded
    dest0 = starts[ids[0]] + ids[2]
    dest1 = starts[ids[1]] + ids[3]
    n_blocks = (2 * n_rows + bs - 1) // bs + N_EXPERTS
    block_start = jnp.arange(n_blocks, dtype=I32) * bs
    block_expert = jnp.minimum(jnp.sum((pad_ends[None, :] <= block_start[:, None]).astype(I32), axis=1),
                               N_EXPERTS - 1)
    n_used = (pad_ends[-1:] // bs).astype(I32)
    buf = _dispatch(tok, dest0, dest1, n_rows=n_rows, n_slots=n_blocks * bs)
    y = _ffn(buf, w1, w3, w2, block_expert, n_used, tm=bs, n_blocks=n_blocks)
    tc = 256
    return _combine(dest0, dest1, y, xs, gates[0:2].T, mod_l, gn_final, tc=tc, n_tiles=n_rows // tc,
                    n_lat_tiles=n_lat_tiles * (tm // tc), tiles_per_batch=s_len // tc, nb=nb, final=final)


def kernel(x, c, ctx, c_ctx, w_mod, b_mod, norm_mix, norm_ffn, norm_final, w_in, attn_sink, ret_decay_fwd,
           ret_decay_bwd, diff_lambda, diff_norm, w_branch, w_out, ffn_w1, ffn_w3, ffn_w2, moe_router, moe_w1,
           moe_w3, moe_w2):
    nb, s_len, d = x.shape
    c_len = ctx.shape[1]
    depth = w_in.shape[0]
    assert d == D_MODEL and nb < MOD_ROWS and s_len % 256 == 0 and s_len % c_len == 0 and c_len % RET_CHUNK == 0
    n_lat, n_ctx = nb * s_len, nb * c_len
    tm = ROW_TILE
    assert s_len % tm == 0 and n_ctx % tm == 0
    n_lat_tiles, n_all_tiles, tiles_per_batch = n_lat // tm, (n_lat + n_ctx) // tm, s_len // tm
    tile_kw = dict(tm=tm, n_lat_tiles=n_lat_tiles, tiles_per_batch=tiles_per_batch, nb=nb)

    cc = jnp.zeros((MOD_ROWS, d), F32).at[:nb].set(c).at[nb].set(c_ctx)
    mod = _modulation(cc, w_mod, b_mod)
    ptm = PROJ_ROW_TILE if s_len % PROJ_ROW_TILE == 0 and n_ctx % PROJ_ROW_TILE == 0 else tm
    proj_kw = dict(tm=ptm, n_tiles=(n_lat + n_ctx) // ptm, n_lat_tiles=n_lat // ptm, tiles_per_batch=s_len // ptm,
                   nb=nb)
    tabs = _rope_tables(s_len, ptm)
    xs = jnp.concatenate([x.reshape(n_lat, d), ctx.reshape(n_ctx, d)], axis=0)
    one_block = jnp.zeros((n_all_tiles,), I32)
    row = lambda v: v.reshape(1, -1).astype(F32)

    for l in range(depth):
        ctx_out = l < depth - 1
        lam_init = 0.8 - 0.6 * math.exp(-0.3 * l)
        n_tiles = n_all_tiles if ctx_out else n_lat_tiles
        proj = _project(xs, row(norm_mix[l]), mod[l], _proj_weight(w_in[l]), tabs, **proj_kw)

        sink_tab = jnp.broadcast_to(attn_sink[l].astype(F32)[:, None], (N_HEADS_A, LANES))
        dec_tab = jnp.broadcast_to(jnp.stack([ret_decay_fwd[l], ret_decay_bwd[l]]).astype(F32)[:, :, None],
                                   (2, N_HEADS_B, LANES))
        lam_p, gain = diff_lambda[l].astype(F32), row(diff_norm[l])
        kw = dict(nb=nb, s_len=s_len, c_len=c_len)
        zero_state = jnp.zeros((nb, N_HEADS_B, 2, LANES, LANES), F32)
        oa, ob, oc = (jnp.zeros((n_lat + n_ctx, BRANCH_W), BF16) for _ in range(N_BRANCH))
        ob, state = _retention(proj, dec_tab, zero_state, ob, latent=False, **kw)
        ob, _ = _retention(proj, dec_tab, state, ob, latent=True, **kw)
        if ctx_out:
            oa = _gqa(proj, sink_tab, oa, local=False, **kw)
            oc = _diff_attention(proj, lam_p, gain, oc, lam_init=lam_init, latent=False, **kw)
        oa = _gqa(proj, sink_tab, oa, local=True, **kw)
        oc = _diff_attention(proj, lam_p, gain, oc, lam_init=lam_init, latent=True, **kw)

        is_moe = l % 2 == 1
        xs, tok = _merge(oa, ob, oc, proj, w_branch[l].astype(BF16), w_out[l].astype(BF16), xs, mod[l],
                         row(norm_ffn[l]), n_tiles=n_tiles, tok_dtype=F32 if is_moe else BF16, **tile_kw)
        final = l == depth - 1
        if is_moe:
            e = l // 2
            xs = _moe(tok, xs, mod[l], moe_router[e], moe_w1[e].astype(BF16), moe_w3[e].astype(BF16),
                      moe_w2[e].astype(BF16), row(norm_final), n_rows=n_tiles * tm, tm=tm,
                      n_lat_tiles=n_lat_tiles, nb=nb, s_len=s_len, final=final)
        else:
            e = l // 2
            xs = _ffn(tok, ffn_w1[e:e + 1].astype(BF16), ffn_w3[e:e + 1].astype(BF16), ffn_w2[e:e + 1].astype(BF16),
                      one_block, jnp.full((1,), n_tiles, I32), n_blocks=n_tiles, xs=xs, mod_l=mod[l], **tile_kw)
            assert not final
    return xs[:n_lat].reshape(nb, s_len, d)
```

```python
import functools
import math

import jax
import jax.numpy as jnp
from jax import lax
from jax.experimental import pallas as pl
from jax.experimental.pallas import tpu as pltpu

F32 = jnp.float32
BF16 = jnp.bfloat16
I32 = jnp.int32

D_MODEL = 1024
GRID_W = 64
HEAD_DIM = 64
N_HEADS_A = 8
N_KV_A = 2
REP_A = N_HEADS_A // N_KV_A
WINDOW = 128
ATTN_BLOCK = 128
N_HEADS_B = 4
RET_CHUNK = 256
N_HEADS_C = 4
N_BRANCH = 3
BRANCH_W = 512
D_FF = 2816
N_EXPERTS = 8
ROPE_BASE = 10000.0
ROPE_FREQS = HEAD_DIM // 4
NORM_EPS = 1e-6
NEG_INF = -1e30
LOG2E = math.log2(math.e)

LANES = 128
MOD_ROWS = 16
G_QA, G_KA, G_QB, G_KB, G_QD, G_KD = 0, 4, 6, 8, 10, 14
N_ROPE_GROUPS = 18
G_VA, G_VB, G_GB, G_VD, G_GL = 18, 20, 24, 28, 32
PROJ_W = 56 * LANES
M_SH1, M_SC1, M_G1, M_SH2, M_SC2, M_G2 = range(6)

FF_TILE = 1408
ROW_TILE = 512
PROJ_ROW_TILE = 1024
DIFF_Q_TILE = 512
DIFF_K_TILE = 256
VMEM_LIMIT = 56 * 1024 * 1024


def _cparams(sem):
    return pltpu.CompilerParams(dimension_semantics=sem, vmem_limit_bytes=VMEM_LIMIT)


def _sigmoid(x):
    return 1.0 / (1.0 + jnp.exp(-x))


def _nt_dot(a, b):
    return lax.dot_general(a, b, (((1,), (1,)), ((), ())), preferred_element_type=F32)


def _head_slot(lane):
    return (lane // 32) % 2


def _rms_rows(x):
    return x * lax.rsqrt(jnp.mean(x * x, axis=-1, keepdims=True) + NORM_EPS)


def _mod_kernel(c_ref, w_ref, b_ref, o_ref):
    c = c_ref[...]
    s = c * _sigmoid(c)
    o_ref[0] = jnp.dot(s, w_ref[0], preferred_element_type=F32,
                       precision=lax.Precision.HIGHEST) + b_ref[0]


def _modulation(cc, w_mod, b_mod):
    depth, d, n = w_mod.shape
    tn = 1536
    return pl.pallas_call(
        _mod_kernel,
        grid=(depth, n // tn),
        in_specs=[pl.BlockSpec((MOD_ROWS, d), lambda l, j: (0, 0)),
                  pl.BlockSpec((1, d, tn), lambda l, j: (l, 0, j)),
                  pl.BlockSpec((1, 1, tn), lambda l, j: (l, 0, j))],
        out_specs=pl.BlockSpec((1, MOD_ROWS, tn), lambda l, j: (l, 0, j)),
        out_shape=jax.ShapeDtypeStruct((depth, MOD_ROWS, n), F32),
        compiler_params=_cparams(("arbitrary", "arbitrary")),
        name="modulation",
    )(cc, w_mod, b_mod.reshape(depth, 1, n))


def _mod_spec(chunk, nargs):
    if nargs == 1:
        return pl.BlockSpec((MOD_ROWS, D_MODEL), lambda i: (0, chunk))
    if nargs == 2:
        return pl.BlockSpec((MOD_ROWS, D_MODEL), lambda i, j: (0, chunk))
    return pl.BlockSpec((MOD_ROWS, D_MODEL), lambda i, j, *_: (0, chunk))


def _segment(i, n_lat_tiles, tiles_per_batch, nb):
    return jnp.where(i < n_lat_tiles, i // tiles_per_batch, nb)


def _proj_kernel(x_ref, gn_ref, sh_ref, sc_ref, w_ref, cos_ref, sin_ref, o_ref, h_ref, *,
                 n_lat_tiles, tiles_per_batch, nb, groups):
    i = pl.program_id(0)
    j = pl.program_id(1)

    @pl.when(j == 0)
    def _():
        seg = _segment(i, n_lat_tiles, tiles_per_batch, nb)
        sc = sc_ref[pl.ds(seg, 1), :]
        sh = sh_ref[pl.ds(seg, 1), :]
        h = (_rms_rows(x_ref[...]) * gn_ref[...]) * (1.0 + sc) + sh
        h_ref[...] = h.astype(BF16)

    acc = jnp.dot(h_ref[...], w_ref[...], preferred_element_type=F32)
    for g in range(groups):
        cols = slice(g * LANES, (g + 1) * LANES)
        a = acc[:, cols]
        gi = j * groups + g

        @pl.when(gi < N_ROPE_GROUPS)
        def _():
            y = a * cos_ref[...] + pltpu.roll(a, LANES // 2, 1) * sin_ref[...]
            o_ref[:, cols] = y.astype(BF16)

        @pl.when(gi >= N_ROPE_GROUPS)
        def _():
            o_ref[:, cols] = a.astype(BF16)


def _project(xs, gn, mod_l, w, tabs, *, tm, n_tiles, n_lat_tiles, tiles_per_batch, nb):
    tn = 1024
    groups = tn // LANES
    rope_map = lambda i, j: (jnp.where(i < n_lat_tiles, i % tiles_per_batch, tiles_per_batch), 0)
    kern = functools.partial(_proj_kernel, n_lat_tiles=n_lat_tiles, tiles_per_batch=tiles_per_batch,
                             nb=nb, groups=groups)
    return pl.pallas_call(
        kern,
        grid=(n_tiles, PROJ_W // tn),
        in_specs=[pl.BlockSpec((tm, D_MODEL), lambda i, j: (i, 0)),
                  pl.BlockSpec((1, D_MODEL), lambda i, j: (0, 0)),
                  _mod_spec(M_SH1, 2), _mod_spec(M_SC1, 2),
                  pl.BlockSpec((D_MODEL, tn), lambda i, j: (0, j)),
                  pl.BlockSpec((tm, LANES), rope_map),
                  pl.BlockSpec((tm, LANES), rope_map)],
        out_specs=pl.BlockSpec((tm, tn), lambda i, j: (i, j)),
        out_shape=jax.ShapeDtypeStruct((xs.shape[0], PROJ_W), BF16),
        scratch_shapes=[pltpu.VMEM((tm, D_MODEL), BF16)],
        compiler_params=_cparams(("arbitrary", "arbitrary")),
        name="in_proj",
    )(xs, gn, mod_l, mod_l, w, *tabs)


def _rope_tables(s, tm):
    rows = s // GRID_W
    row = jnp.repeat(jnp.arange(rows, dtype=F32), GRID_W)
    col = jnp.tile(jnp.arange(GRID_W, dtype=F32), rows)
    inv = 1.0 / (ROPE_BASE ** (jnp.arange(ROPE_FREQS, dtype=F32) / ROPE_FREQS))
    ar, ac = row[:, None] * inv, col[:, None] * inv
    cos = jnp.tile(jnp.concatenate([jnp.cos(ar), jnp.cos(ac)], axis=1), (1, 4))
    sin = jnp.tile(jnp.concatenate([jnp.sin(ar), jnp.sin(ac)], axis=1), (1, 2))
    sin = jnp.concatenate([-sin, sin], axis=1)
    ident = lambda t, v: jnp.concatenate([t, jnp.full((tm, LANES), v, F32)], axis=0)
    return ident(cos, 1.0), ident(sin, 0.0)


def _proj_weight(w):
    sl = lambda a, b: w[:, a:b]
    sc = HEAD_DIM ** -0.5
    dup = lambda a: jnp.concatenate([sl(a, a + 64), sl(a, a + 64), sl(a + 64, a + 128), sl(a + 64, a + 128)], axis=1)
    parts = [sl(0, 512) * (sc * LOG2E), dup(512), sl(768, 1024), sl(1024, 1280) * sc, sl(2304, 2816) * (sc * LOG2E),
             sl(2816, 3328), dup(640), sl(1280, 1792), sl(1792, 2304), sl(3328, 3840), sl(3840, 6912)]
    n_rope = N_ROPE_GROUPS * LANES
    rope = jnp.concatenate(parts[:6], axis=1).reshape(w.shape[0], N_ROPE_GROUPS, 2, 2, 2, ROPE_FREQS)
    rope = rope.transpose(0, 1, 4, 2, 3, 5).reshape(w.shape[0], n_rope)
    return jnp.concatenate([rope] + parts[6:], axis=1).astype(BF16)


def _gqa_kernel(*refs, local, s_len, tq):
    if local:
        (q_ref, kp_ref, kc_ref, kn_ref, kx_ref, vp_ref, vc_ref, vn_ref, vx_ref, sink_ref, _, o_ref) = refs
    else:
        q_ref, kx_ref, vx_ref, sink_ref, _, o_ref = refs
    lane = lax.broadcasted_iota(I32, (tq, LANES), 1)
    lo = lane < HEAD_DIM
    slot0 = _head_slot(lane) == 0
    if local:
        nk = 3 * tq + kx_ref.shape[0]
        qi = lax.broadcasted_iota(I32, (tq, nk), 0)
        kj = lax.broadcasted_iota(I32, (tq, nk), 1)
        rel = kj - qi
        kpos = pl.program_id(1) * tq - WINDOW + kj
        bad = ((rel < 0) | (rel > 2 * WINDOW) | (kpos < 0) | (kpos >= s_len)) & (kj < 3 * tq)
    for g in range(N_KV_A):
        cols = slice(g * LANES, (g + 1) * LANES)
        if local:
            kcat = jnp.concatenate([kp_ref[:, cols], kc_ref[:, cols], kn_ref[:, cols], kx_ref[:, cols]], axis=0)
            vcat = jnp.concatenate([vp_ref[:, cols], vc_ref[:, cols], vn_ref[:, cols], vx_ref[:, cols]], axis=0)
        else:
            kcat, vcat = kx_ref[:, cols], vx_ref[:, cols]
        qs = []
        for r in range(REP_A):
            h = g * REP_A + r
            qp = q_ref[:, (h // 2) * LANES:(h // 2 + 1) * LANES]
            qs.append(jnp.where(slot0 if h % 2 == 0 else jnp.logical_not(slot0), qp, jnp.zeros_like(qp)))
        s = _nt_dot(jnp.concatenate(qs, axis=0), kcat)
        ps, dens = [], []
        for r in range(REP_A):
            h = g * REP_A + r
            sr = s[r * tq:(r + 1) * tq]
            if local:
                sr = jnp.where(bad, NEG_INF, sr)
            sk = sink_ref[h:h + 1, 0:1] * LOG2E
            m = jnp.maximum(jnp.max(sr, axis=-1, keepdims=True), sk)
            p = jnp.exp2(sr - m)
            dens.append(jnp.sum(p, axis=-1, keepdims=True) + jnp.exp2(sk - m))
            ps.append(p.astype(BF16))
        o = jnp.dot(jnp.concatenate(ps, axis=0), vcat, preferred_element_type=F32)
        outs = [o[r * tq:(r + 1) * tq] / dens[r] for r in range(REP_A)]
        for pr in range(REP_A // 2):
            pair = jnp.where(lo, outs[2 * pr], outs[2 * pr + 1])
            c0 = (g * (REP_A // 2) + pr) * LANES
            o_ref[:, c0:c0 + LANES] = pair.astype(BF16)


def _gqa(proj, sink_tab, o_prev, *, nb, s_len, c_len, local):
    n_rows = proj.shape[0]
    cblk = nb * s_len // c_len
    kcol, vcol = G_KA // 2, G_VA // 2
    out_shape = jax.ShapeDtypeStruct((n_rows, N_HEADS_A * HEAD_DIM), BF16)
    if local:
        tq = ATTN_BLOCK
        nq = s_len // tq
        row = lambda b, i: b * nq + i
        specs = [pl.BlockSpec((tq, 4 * LANES), lambda b, i: (row(b, i), 0))]
        for col in (kcol, vcol):
            specs += [pl.BlockSpec((tq, 2 * LANES), lambda b, i, col=col: (row(b, jnp.maximum(i - 1, 0)), col)),
                      pl.BlockSpec((tq, 2 * LANES), lambda b, i, col=col: (row(b, i), col)),
                      pl.BlockSpec((tq, 2 * LANES), lambda b, i, col=col: (row(b, jnp.minimum(i + 1, nq - 1)), col)),
                      pl.BlockSpec((c_len, 2 * LANES), lambda b, i, col=col: (cblk + b, col))]
        specs += [pl.BlockSpec((N_HEADS_A, LANES), lambda b, i: (0, 0)),
                  pl.BlockSpec(memory_space=pl.ANY)]
        args = [proj] + [proj] * 8 + [sink_tab, o_prev]
        return pl.pallas_call(
            functools.partial(_gqa_kernel, local=True, s_len=s_len, tq=tq),
            grid=(nb, nq), in_specs=specs,
            out_specs=pl.BlockSpec((tq, 4 * LANES), lambda b, i: (row(b, i), 0)),
            out_shape=out_shape, input_output_aliases={10: 0},
            compiler_params=_cparams(("arbitrary", "arbitrary")), name="gqa_window",
        )(*args)
    specs = [pl.BlockSpec((c_len, 4 * LANES), lambda b: (cblk + b, 0)),
             pl.BlockSpec((c_len, 2 * LANES), lambda b: (cblk + b, kcol)),
             pl.BlockSpec((c_len, 2 * LANES), lambda b: (cblk + b, vcol)),
             pl.BlockSpec((N_HEADS_A, LANES), lambda b: (0, 0)),
             pl.BlockSpec(memory_space=pl.ANY)]
    return pl.pallas_call(
        functools.partial(_gqa_kernel, local=False, s_len=s_len, tq=c_len),
        grid=(nb,), in_specs=specs,
        out_specs=pl.BlockSpec((c_len, 4 * LANES), lambda b: (cblk + b, 0)),
        out_shape=out_shape, input_output_aliases={4: 0},
        compiler_params=_cparams(("arbitrary",)), name="gqa_context",
    )(proj, proj, proj, sink_tab, o_prev)


def _ret_kernel(q_ref, k_ref, v_ref, g_ref, dec_ref, sin_ref, _, o_ref, sout_ref, of_ref, ob_ref, *, n_rows):
    h = pl.program_id(1)
    c = RET_CHUNK
    nch = n_rows // c
    lane = lax.broadcasted_iota(I32, (c, LANES), 1)
    row = lax.broadcasted_iota(I32, (c, LANES), 0)
    qmask = _head_slot(lane) == (h % 2)

    def log_sigmoid(d):
        x = dec_ref[d, pl.ds(h, 1), :]
        return jnp.minimum(x, 0.0) - jnp.log(1.0 + jnp.exp(-jnp.abs(x)))

    lg_f, lg_b = log_sigmoid(0), log_sigmoid(1)
    diff = lax.broadcasted_iota(I32, (c, c), 0) - lax.broadcasted_iota(I32, (c, c), 1)
    rowf = row.astype(F32)
    dmat_f = jnp.where(diff >= 0, jnp.exp(jnp.maximum(diff, 0).astype(F32) * lg_f[:, 0:1]), 0.0)
    dmat_b = jnp.where(diff <= 0, jnp.exp(jnp.maximum(-diff, 0).astype(F32) * lg_b[:, 0:1]), 0.0)
    qdec_f = jnp.exp((rowf + 1.0) * lg_f)
    kdec_f = jnp.exp((c - 1.0 - rowf) * lg_f)
    qdec_b = jnp.exp((c - rowf) * lg_b)
    kdec_b = jnp.exp(rowf * lg_b)
    cdec_f = jnp.exp(c * lg_f)
    cdec_b = jnp.exp(c * lg_b)

    def chunk(n, state, dmat, qdec, kdec, cdec, dst_ref):
        rows = pl.ds(pl.multiple_of(n * c, c), c)
        q = q_ref[rows, :]
        q = jnp.where(qmask, q, jnp.zeros_like(q))
        k = k_ref[rows, :]
        v = v_ref[rows, :]
        sc = (_nt_dot(q, k) * dmat).astype(BF16)
        intra = jnp.dot(sc, v, preferred_element_type=F32)
        inter = jnp.dot((q.astype(F32) * qdec).astype(BF16), state.astype(BF16), preferred_element_type=F32)
        dst_ref[rows, :] = intra + inter
        kd = (k.astype(F32) * kdec).T.astype(BF16)
        return state * cdec + jnp.dot(kd, v, preferred_element_type=F32)

    def body(n, carry):
        sf, sb = carry
        sf = chunk(n, sf, dmat_f, qdec_f, kdec_f, cdec_f, of_ref)
        sb = chunk(nch - 1 - n, sb, dmat_b, qdec_b, kdec_b, cdec_b, ob_ref)
        return sf, sb

    sf, sb = lax.fori_loop(0, nch, body, (sin_ref[0, 0, 0], sin_ref[0, 0, 1]), unroll=2)
    sout_ref[0, 0, 0] = sf
    sout_ref[0, 0, 1] = sb
    o = of_ref[...] + ob_ref[...]
    gate = g_ref[...].astype(F32)
    o_ref[...] = (_rms_rows(o) * (gate * _sigmoid(gate))).astype(BF16)


def _retention(proj, dec_tab, s_in, o_prev, *, nb, s_len, c_len, latent):
    n_rows = s_len if latent else c_len
    base = 0 if latent else nb * s_len // c_len
    rb = lambda b: base + b
    col = lambda g0, per_pair: (lambda b, h: (rb(b), g0 + (h // 2 if per_pair else h)))
    state_spec = pl.BlockSpec((1, 1, 2, LANES, LANES), lambda b, h: (b, h, 0, 0, 0))
    specs = [pl.BlockSpec((n_rows, LANES), col(G_QB, True)),
             pl.BlockSpec((n_rows, LANES), col(G_KB, True)),
             pl.BlockSpec((n_rows, LANES), col(G_VB, False)),
             pl.BlockSpec((n_rows, LANES), col(G_GB, False)),
             pl.BlockSpec((2, N_HEADS_B, LANES), lambda b, h: (0, 0, 0)),
             state_spec, pl.BlockSpec(memory_space=pl.ANY)]
    args = [proj, proj, proj, proj, dec_tab, s_in, o_prev]
    aliases = {6: 0}
    return pl.pallas_call(
        functools.partial(_ret_kernel, n_rows=n_rows),
        grid=(nb, N_HEADS_B), in_specs=specs,
        out_specs=[pl.BlockSpec((n_rows, LANES), lambda b, h: (rb(b), h)), state_spec],
        out_shape=[jax.ShapeDtypeStruct((proj.shape[0], N_HEADS_B * LANES), BF16),
                   jax.ShapeDtypeStruct(s_in.shape, F32)],
        scratch_shapes=[pltpu.VMEM((n_rows, LANES), F32), pltpu.VMEM((n_rows, LANES), F32)],
        input_output_aliases=aliases,
        compiler_params=_cparams(("arbitrary", "arbitrary")),
        name="retention_latent" if latent else "retention_context",
    )(*args)


def _diff_kernel(*refs, latent, lam_init):
    if latent:
        q_ref, kl_ref, kx_ref, vl_ref, vx_ref, lam_ref, gain_ref, _, o_ref, k_scr, v_scr = refs
    else:
        q_ref, kx_ref, vx_ref, lam_ref, gain_ref, _, o_ref, k_scr, v_scr = refs
    nl = kl_ref.shape[0] if latent else 0

    def fill():
        if latent:
            k_scr[0:nl, :] = kl_ref[...]
            v_scr[0:nl, 0:LANES] = vl_ref[...]
        k_scr[nl:, :] = kx_ref[...]
        v_scr[nl:, 0:LANES] = vx_ref[...]
        v_scr[:, LANES:] = jnp.ones((v_scr.shape[0], LANES), BF16)

    if latent:
        pl.when(pl.program_id(2) == 0)(fill)
    else:
        fill()
    lp = lam_ref[...]
    lam = (jnp.exp(jnp.sum(lp[0:1] * lp[1:2], axis=-1, keepdims=True))
           - jnp.exp(jnp.sum(lp[2:3] * lp[3:4], axis=-1, keepdims=True)) + lam_init)
    q = q_ref[...]
    lo = _head_slot(lax.broadcasted_iota(I32, q.shape, 1)) == 0
    zero = jnp.zeros_like(q)
    n_keys = k_scr.shape[0]
    outs = []
    for comp in range(2):
        qc = jnp.where(lo, q, zero) if comp == 0 else jnp.where(lo, zero, q)
        m = acc = None
        for start in range(0, n_keys, DIFF_K_TILE):
            size = min(DIFF_K_TILE, n_keys - start)
            s = _nt_dot(qc, k_scr[start:start + size, :])
            m_t = jnp.max(s, axis=-1, keepdims=True)
            m_new = m_t if m is None else jnp.maximum(m, m_t)
            p = jnp.exp2(s - m_new).astype(BF16)
            pv = jnp.dot(p, v_scr[start:start + size, :], preferred_element_type=F32)
            acc = pv if acc is None else jnp.exp2(m - m_new) * acc + pv
            m = m_new
        outs.append(acc[:, :LANES] / acc[:, LANES:])
    o = outs[0] - lam * outs[1]
    o_ref[...] = (_rms_rows(o) * gain_ref[...] * (1.0 - lam_init)).astype(BF16)


def _diff_attention(proj, lam_p, gain, o_prev, *, nb, s_len, c_len, lam_init, latent):
    cblk = nb * s_len // c_len
    out_shape = jax.ShapeDtypeStruct((proj.shape[0], N_HEADS_C * LANES), BF16)
    small = [pl.BlockSpec((4, HEAD_DIM), lambda *_: (0, 0)), pl.BlockSpec((1, LANES), lambda *_: (0, 0))]
    if latent:
        tq = min(DIFF_Q_TILE, s_len)
        nq = s_len // tq
        specs = [pl.BlockSpec((tq, LANES), lambda b, h, i: (b * nq + i, G_QD + h)),
                 pl.BlockSpec((s_len, LANES), lambda b, h, i: (b, G_KD + h)),
                 pl.BlockSpec((c_len, LANES), lambda b, h, i: (cblk + b, G_KD + h)),
                 pl.BlockSpec((s_len, LANES), lambda b, h, i: (b, G_VD + h)),
                 pl.BlockSpec((c_len, LANES), lambda b, h, i: (cblk + b, G_VD + h))] + small
        specs.append(pl.BlockSpec(memory_space=pl.ANY))
        return pl.pallas_call(
            functools.partial(_diff_kernel, latent=True, lam_init=lam_init),
            grid=(nb, N_HEADS_C, nq), in_specs=specs,
            out_specs=pl.BlockSpec((tq, LANES), lambda b, h, i: (b * nq + i, h)),
            out_shape=out_shape, input_output_aliases={7: 0},
            scratch_shapes=[pltpu.VMEM((s_len + c_len, LANES), BF16), pltpu.VMEM((s_len + c_len, 2 * LANES), BF16)],
            compiler_params=_cparams(("arbitrary", "arbitrary", "arbitrary")), name="diff_latent",
        )(proj, proj, proj, proj, proj, lam_p, gain, o_prev)
    specs = [pl.BlockSpec((c_len, LANES), lambda b, h: (cblk + b, G_QD + h)),
             pl.BlockSpec((c_len, LANES), lambda b, h: (cblk + b, G_KD + h)),
             pl.BlockSpec((c_len, LANES), lambda b, h: (cblk + b, G_VD + h))] + small
    specs.append(pl.BlockSpec(memory_space=pl.ANY))
    return pl.pallas_call(
        functools.partial(_diff_kernel, latent=False, lam_init=lam_init),
        grid=(nb, N_HEADS_C), in_specs=specs,
        out_specs=pl.BlockSpec((c_len, LANES), lambda b, h: (cblk + b, h)),
        out_shape=out_shape, input_output_aliases={5: 0},
        scratch_shapes=[pltpu.VMEM((c_len, LANES), BF16), pltpu.VMEM((c_len, 2 * LANES), BF16)],
        compiler_params=_cparams(("arbitrary", "arbitrary")), name="diff_context",
    )(proj, proj, proj, lam_p, gain, o_prev)


def _merge_kernel(oa_ref, ob_ref, oc_ref, ga_ref, gb_ref, gc_ref, wb_ref, wo_ref, x_ref, g1_ref, sh_ref, sc_ref,
                  gn_ref, xo_ref, tok_ref, *, n_lat_tiles, tiles_per_batch, nb):
    i = pl.program_id(0)
    seg = _segment(i, n_lat_tiles, tiles_per_batch, nb)
    y = None
    for k, (o_ref, gl_ref) in enumerate(((oa_ref, ga_ref), (ob_ref, gb_ref), (oc_ref, gc_ref))):
        t = _sigmoid(gl_ref[...].astype(F32)) * jnp.dot(o_ref[...], wb_ref[k], preferred_element_type=F32)
        y = t if y is None else y + t
    z = jnp.dot(y.astype(BF16), wo_ref[...], preferred_element_type=F32)
    xn = x_ref[...] + g1_ref[pl.ds(seg, 1), :] * z
    xo_ref[...] = xn
    tok = (_rms_rows(xn) * gn_ref[...]) * (1.0 + sc_ref[pl.ds(seg, 1), :]) + sh_ref[pl.ds(seg, 1), :]
    tok_ref[...] = tok.astype(tok_ref.dtype)


def _merge(oa, ob, oc, proj, wb, wo, xs, mod_l, gn, *, tm, n_tiles, n_lat_tiles, tiles_per_batch, nb, tok_dtype):
    row = lambda w: pl.BlockSpec((tm, w), lambda i: (i, 0))
    gl = lambda k: pl.BlockSpec((tm, D_MODEL), lambda i: (i, G_GL // 8 + k))
    return pl.pallas_call(
        functools.partial(_merge_kernel, n_lat_tiles=n_lat_tiles, tiles_per_batch=tiles_per_batch, nb=nb),
        grid=(n_tiles,),
        in_specs=[row(BRANCH_W), row(BRANCH_W), row(BRANCH_W), gl(0), gl(1), gl(2),
                  pl.BlockSpec((N_BRANCH, BRANCH_W, D_MODEL), lambda i: (0, 0, 0)),
                  pl.BlockSpec((D_MODEL, D_MODEL), lambda i: (0, 0)),
                  row(D_MODEL), _mod_spec(M_G1, 1), _mod_spec(M_SH2, 1), _mod_spec(M_SC2, 1),
                  pl.BlockSpec((1, D_MODEL), lambda i: (0, 0))],
        out_specs=[row(D_MODEL), row(D_MODEL)],
        out_shape=[jax.ShapeDtypeStruct(xs.shape, F32), jax.ShapeDtypeStruct((n_tiles * tm, D_MODEL), tok_dtype)],
        input_output_aliases={8: 0},
        compiler_params=_cparams(("arbitrary",)), name="merge",
    )(oa, ob, oc, proj, proj, proj, wb, wo, xs, mod_l, mod_l, mod_l, gn)


def _ffn_kernel(be_ref, nu_ref, *refs, dense, n_lat_tiles, tiles_per_batch, nb):
    if dense:
        t_ref, w1_ref, w3_ref, w2_ref, x_ref, g2_ref, o_ref, acc_ref = refs
    else:
        t_ref, w1_ref, w3_ref, w2_ref, o_ref, acc_ref = refs
    i = pl.program_id(0)
    f = pl.program_id(1)
    last = pl.num_programs(1) - 1
    used = i < nu_ref[0]

    @pl.when(used)
    def _():
        t = t_ref[...].astype(BF16)
        h1 = jnp.dot(t, w1_ref[0], preferred_element_type=F32)
        h3 = jnp.dot(t, w3_ref[0], preferred_element_type=F32)
        a = ((h1 * _sigmoid(h1)) * h3).astype(BF16)
        part = jnp.dot(a, w2_ref[0], preferred_element_type=F32)

        @pl.when(f == 0)
        def _():
            acc_ref[...] = part

        @pl.when(f > 0)
        def _():
            acc_ref[...] += part

    @pl.when(f == last)
    def _():
        if dense:
            seg = _segment(i, n_lat_tiles, tiles_per_batch, nb)
            o_ref[...] = x_ref[...] + g2_ref[pl.ds(seg, 1), :] * acc_ref[...]
        else:
            o_ref[...] = jnp.where(used, acc_ref[...], 0.0)


def _ffn(tok, w1, w3, w2, block_expert, n_used, *, tm, n_blocks, xs=None, mod_l=None,
         n_lat_tiles=0, tiles_per_batch=1, nb=0):
    dense = xs is not None
    nf = D_FF // FF_TILE
    specs = [pl.BlockSpec((tm, D_MODEL), lambda i, f, be, nu: (i, 0)),
             pl.BlockSpec((1, D_MODEL, FF_TILE), lambda i, f, be, nu: (be[i], 0, f)),
             pl.BlockSpec((1, D_MODEL, FF_TILE), lambda i, f, be, nu: (be[i], 0, f)),
             pl.BlockSpec((1, FF_TILE, D_MODEL), lambda i, f, be, nu: (be[i], f, 0))]
    args = [tok, w1, w3, w2]
    aliases = {}
    if dense:
        specs += [pl.BlockSpec((tm, D_MODEL), lambda i, f, be, nu: (i, 0)), _mod_spec(M_G2, 4)]
        args += [xs, mod_l]
        aliases = {6: 0}
        out_rows = xs.shape[0]
    else:
        out_rows = tok.shape[0]
    return pl.pallas_call(
        functools.partial(_ffn_kernel, dense=dense, n_lat_tiles=n_lat_tiles, tiles_per_batch=tiles_per_batch, nb=nb),
        grid_spec=pltpu.PrefetchScalarGridSpec(
            num_scalar_prefetch=2, grid=(n_blocks, nf), in_specs=specs,
            out_specs=pl.BlockSpec((tm, D_MODEL), lambda i, f, be, nu: (i, 0)),
            scratch_shapes=[pltpu.VMEM((tm, D_MODEL), F32)]),
        out_shape=jax.ShapeDtypeStruct((out_rows, D_MODEL), F32),
        input_output_aliases=aliases,
        compiler_params=_cparams(("arbitrary", "arbitrary")),
        name="ffn_dense" if dense else "ffn_experts",
    )(block_expert, n_used, *args)


def _router_kernel(tok_ref, wr_ref, ids_ref, gate_ref, cnt_ref, base_ref, *, tr):
    i = pl.program_id(0)

    @pl.when(i == 0)
    def _():
        base_ref[...] = jnp.zeros_like(base_ref)

    logits = _nt_dot(wr_ref[...], tok_ref[...].astype(BF16))
    eidx = lax.broadcasted_iota(I32, logits.shape, 0)
    m1 = jnp.max(logits, axis=0, keepdims=True)
    i1 = jnp.min(jnp.where(logits == m1, eidx, N_EXPERTS), axis=0, keepdims=True)
    rest = jnp.where(eidx == i1, -jnp.inf, logits)
    m2 = jnp.max(rest, axis=0, keepdims=True)
    i2 = jnp.min(jnp.where(rest == m2, eidx, N_EXPERTS), axis=0, keepdims=True)
    e = jnp.exp(m2 - m1)
    sel1 = eidx == i1
    sel2 = eidx == i2
    both = jnp.where(sel1 | sel2, 1.0, 0.0)
    before = lax.broadcasted_iota(I32, (tr, tr), 0) < lax.broadcasted_iota(I32, (tr, tr), 1)
    prefix = jnp.dot(both.astype(BF16), jnp.where(before, 1.0, 0.0).astype(BF16), preferred_element_type=F32)
    pos = base_ref[:, 0:1] + prefix
    r1 = jnp.sum(jnp.where(sel1, pos, 0.0), axis=0, keepdims=True)
    r2 = jnp.sum(jnp.where(sel2, pos, 0.0), axis=0, keepdims=True)
    ids_ref[0:1, :] = i1
    ids_ref[1:2, :] = i2
    ids_ref[2:3, :] = r1.astype(I32)
    ids_ref[3:4, :] = r2.astype(I32)
    ids_ref[4:8, :] = jnp.zeros((4, tr), I32)
    gate_ref[0:1, :] = 1.0 / (1.0 + e)
    gate_ref[1:2, :] = e / (1.0 + e)
    gate_ref[2:8, :] = jnp.zeros((6, tr), F32)
    base_ref[...] = base_ref[...] + jnp.sum(both, axis=1, keepdims=True)
    cnt_ref[...] = base_ref[...].astype(I32)


def _router(tok, wr, *, n_rows, tr):
    return pl.pallas_call(
        functools.partial(_router_kernel, tr=tr),
        grid=(n_rows // tr,),
        in_specs=[pl.BlockSpec((tr, D_MODEL), lambda i: (i, 0)),
                  pl.BlockSpec((N_EXPERTS, D_MODEL), lambda i: (0, 0))],
        out_specs=[pl.BlockSpec((8, tr), lambda i: (0, i)), pl.BlockSpec((8, tr), lambda i: (0, i)),
                   pl.BlockSpec((N_EXPERTS, LANES), lambda i: (0, 0))],
        out_shape=[jax.ShapeDtypeStruct((8, n_rows), I32), jax.ShapeDtypeStruct((8, n_rows), F32),
                   jax.ShapeDtypeStruct((N_EXPERTS, LANES), I32)],
        scratch_shapes=[pltpu.VMEM((N_EXPERTS, LANES), F32)],
        compiler_params=_cparams(("arbitrary",)), name="router",
    )(tok, wr)


def _scatter_kernel(d0_ref, d1_ref, tok_ref, _, o_hbm, sem, *, tb):
    i = pl.program_id(0)

    def issue(r, carry):
        t = i * tb + r
        src = tok_ref.at[pl.ds(r, 1), :]
        pltpu.make_async_copy(src, o_hbm.at[pl.ds(d0_ref[t], 1), :], sem).start()
        pltpu.make_async_copy(src, o_hbm.at[pl.ds(d1_ref[t], 1), :], sem).start()
        return carry

    lax.fori_loop(0, tb, issue, 0, unroll=8)
    pltpu.make_async_copy(tok_ref, o_hbm.at[pl.ds(0, tb), :], sem).wait()
    pltpu.make_async_copy(tok_ref, o_hbm.at[pl.ds(0, tb), :], sem).wait()


def _dispatch(tok, dest0, dest1, *, n_rows, n_slots):
    tb = ROW_TILE
    return pl.pallas_call(
        functools.partial(_scatter_kernel, tb=tb),
        grid_spec=pltpu.PrefetchScalarGridSpec(
            num_scalar_prefetch=2, grid=(n_rows // tb,),
            in_specs=[pl.BlockSpec((tb, D_MODEL), lambda i, d0, d1: (i, 0)), pl.BlockSpec(memory_space=pl.ANY)],
            out_specs=pl.BlockSpec(memory_space=pl.ANY),
            scratch_shapes=[pltpu.SemaphoreType.DMA(())]),
        out_shape=jax.ShapeDtypeStruct((n_slots, D_MODEL), tok.dtype),
        input_output_aliases={3: 0},
        compiler_params=_cparams(("arbitrary",)), name="moe_dispatch",
    )(dest0, dest1, tok, jnp.zeros((n_slots, D_MODEL), tok.dtype))


def _combine_kernel(d0_ref, d1_ref, y_hbm, x_ref, gate_ref, g2_ref, gn_ref, o_ref, buf_ref, sem, *,
                    tc, n_lat_tiles, tiles_per_batch, nb, final):
    i = pl.program_id(0)

    def issue(r, carry):
        pltpu.make_async_copy(y_hbm.at[pl.ds(d0_ref[i * tc + r], 1), :], buf_ref.at[0, pl.ds(r, 1), :], sem).start()
        pltpu.make_async_copy(y_hbm.at[pl.ds(d1_ref[i * tc + r], 1), :], buf_ref.at[1, pl.ds(r, 1), :], sem).start()
        return carry

    lax.fori_loop(0, tc, issue, 0, unroll=8)
    pltpu.make_async_copy(y_hbm.at[pl.ds(0, tc), :], buf_ref.at[0], sem).wait()
    pltpu.make_async_copy(y_hbm.at[pl.ds(0, tc), :], buf_ref.at[1], sem).wait()
    seg = _segment(i, n_lat_tiles, tiles_per_batch, nb)
    gates = gate_ref[...]
    f = gates[:, 0:1] * buf_ref[0] + gates[:, 1:2] * buf_ref[1]
    xo = x_ref[...] + g2_ref[pl.ds(seg, 1), :] * f
    if final:
        xo = _rms_rows(xo) * gn_ref[...]
    o_ref[...] = xo


def _combine(dest0, dest1, y, xs, gates, mod_l, gn, *, tc, n_tiles, n_lat_tiles, tiles_per_batch, nb, final):
    out_rows = n_tiles * tc
    return pl.pallas_call(
        functools.partial(_combine_kernel, tc=tc, n_lat_tiles=n_lat_tiles, tiles_per_batch=tiles_per_batch,
                          nb=nb, final=final),
        grid_spec=pltpu.PrefetchScalarGridSpec(
            num_scalar_prefetch=2, grid=(n_tiles,),
            in_specs=[pl.BlockSpec(memory_space=pl.ANY),
                      pl.BlockSpec((tc, D_MODEL), lambda i, a, b: (i, 0)),
                      pl.BlockSpec((tc, 2), lambda i, a, b: (i, 0)),
                      _mod_spec(M_G2, 3),
                      pl.BlockSpec((1, D_MODEL), lambda i, a, b: (0, 0))],
            out_specs=pl.BlockSpec((tc, D_MODEL), lambda i, a, b: (i, 0)),
            scratch_shapes=[pltpu.VMEM((2, tc, D_MODEL), F32), pltpu.SemaphoreType.DMA(())]),
        out_shape=jax.ShapeDtypeStruct((out_rows, D_MODEL), F32),
        compiler_params=_cparams(("arbitrary",)), name="moe_combine",
    )(dest0, dest1, y, xs, gates, mod_l, gn)


def _moe(tok, xs, mod_l, w_router, w1, w3, w2, gn_final, *, n_rows, tm, n_lat_tiles, nb, s_len, final):
    bs = ROW_TILE
    ids, gates, counts = _router(tok, w_router.T.astype(BF16), n_rows=n_rows, tr=tm)
    counts = counts[:, 0]
    padded = (counts + bs - 1) // bs * bs
    pad_ends = jnp.cumsum(padded)
    starts = pad_ends - padded
    dest0 = starts[ids[0]] + ids[2]
    dest1 = starts[ids[1]] + ids[3]
    n_blocks = (2 * n_rows + bs - 1) // bs + N_EXPERTS
    block_start = jnp.arange(n_blocks, dtype=I32) * bs
    block_expert = jnp.minimum(jnp.sum((pad_ends[None, :] <= block_start[:, None]).astype(I32), axis=1),
                               N_EXPERTS - 1)
    n_used = (pad_ends[-1:] // bs).astype(I32)
    buf = _dispatch(tok, dest0, dest1, n_rows=n_rows, n_slots=n_blocks * bs)
    y = _ffn(buf, w1, w3, w2, block_expert, n_used, tm=bs, n_blocks=n_blocks)
    tc = 256
    return _combine(dest0, dest1, y, xs, gates[0:2].T, mod_l, gn_final, tc=tc, n_tiles=n_rows // tc,
                    n_lat_tiles=n_lat_tiles * (tm // tc), tiles_per_batch=s_len // tc, nb=nb, final=final)


def kernel(x, c, ctx, c_ctx, w_mod, b_mod, norm_mix, norm_ffn, norm_final, w_in, attn_sink, ret_decay_fwd,
           ret_decay_bwd, diff_lambda, diff_norm, w_branch, w_out, ffn_w1, ffn_w3, ffn_w2, moe_router, moe_w1,
           moe_w3, moe_w2):
    nb, s_len, d = x.shape
    c_len = ctx.shape[1]
    depth = w_in.shape[0]
    assert d == D_MODEL and nb < MOD_ROWS and s_len % 256 == 0 and s_len % c_len == 0 and c_len % RET_CHUNK == 0
    n_lat, n_ctx = nb * s_len, nb * c_len
    tm = ROW_TILE
    assert s_len % tm == 0 and n_ctx % tm == 0
    n_lat_tiles, n_all_tiles, tiles_per_batch = n_lat // tm, (n_lat + n_ctx) // tm, s_len // tm
    tile_kw = dict(tm=tm, n_lat_tiles=n_lat_tiles, tiles_per_batch=tiles_per_batch, nb=nb)

    cc = jnp.zeros((MOD_ROWS, d), F32).at[:nb].set(c).at[nb].set(c_ctx)
    mod = _modulation(cc, w_mod, b_mod)
    ptm = PROJ_ROW_TILE if s_len % PROJ_ROW_TILE == 0 and n_ctx % PROJ_ROW_TILE == 0 else tm
    proj_kw = dict(tm=ptm, n_tiles=(n_lat + n_ctx) // ptm, n_lat_tiles=n_lat // ptm, tiles_per_batch=s_len // ptm,
                   nb=nb)
    tabs = _rope_tables(s_len, ptm)
    xs = jnp.concatenate([x.reshape(n_lat, d), ctx.reshape(n_ctx, d)], axis=0)
    one_block = jnp.zeros((n_all_tiles,), I32)
    row = lambda v: v.reshape(1, -1).astype(F32)

    for l in range(depth):
        ctx_out = l < depth - 1
        lam_init = 0.8 - 0.6 * math.exp(-0.3 * l)
        n_tiles = n_all_tiles if ctx_out else n_lat_tiles
        proj = _project(xs, row(norm_mix[l]), mod[l], _proj_weight(w_in[l]), tabs, **proj_kw)

        sink_tab = jnp.broadcast_to(attn_sink[l].astype(F32)[:, None], (N_HEADS_A, LANES))
        dec_tab = jnp.broadcast_to(jnp.stack([ret_decay_fwd[l], ret_decay_bwd[l]]).astype(F32)[:, :, None],
                                   (2, N_HEADS_B, LANES))
        lam_p, gain = diff_lambda[l].astype(F32), row(diff_norm[l])
        kw = dict(nb=nb, s_len=s_len, c_len=c_len)
        zero_state = jnp.zeros((nb, N_HEADS_B, 2, LANES, LANES), F32)
        oa, ob, oc = (jnp.zeros((n_lat + n_ctx, BRANCH_W), BF16) for _ in range(N_BRANCH))
        ob, state = _retention(proj, dec_tab, zero_state, ob, latent=False, **kw)
        ob, _ = _retention(proj, dec_tab, state, ob, latent=True, **kw)
        if ctx_out:
            oa = _gqa(proj, sink_tab, oa, local=False, **kw)
            oc = _diff_attention(proj, lam_p, gain, oc, lam_init=lam_init, latent=False, **kw)
        oa = _gqa(proj, sink_tab, oa, local=True, **kw)
        oc = _diff_attention(proj, lam_p, gain, oc, lam_init=lam_init, latent=True, **kw)

        is_moe = l % 2 == 1
        xs, tok = _merge(oa, ob, oc, proj, w_branch[l].astype(BF16), w_out[l].astype(BF16), xs, mod[l],
                         row(norm_ffn[l]), n_tiles=n_tiles, tok_dtype=F32 if is_moe else BF16, **tile_kw)
        final = l == depth - 1
        if is_moe:
            e = l // 2
            xs = _moe(tok, xs, mod[l], moe_router[e], moe_w1[e].astype(BF16), moe_w3[e].astype(BF16),
                      moe_w2[e].astype(BF16), row(norm_final), n_rows=n_tiles * tm, tm=tm,
                      n_lat_tiles=n_lat_tiles, nb=nb, s_len=s_len, final=final)
        else:
            e = l // 2
            xs = _ffn(tok, ffn_w1[e:e + 1].astype(BF16), ffn_w3[e:e + 1].astype(BF16), ffn_w2[e:e + 1].astype(BF16),
                      one_block, jnp.full((1,), n_tiles, I32), n_blocks=n_tiles, xs=xs, mod_l=mod[l], **tile_kw)
            assert not final
    return xs[:n_lat].reshape(nb, s_len, d)
```

```python
import functools
import math

import jax
import jax.numpy as jnp
from jax import lax
from jax.experimental import pallas as pl
from jax.experimental.pallas import tpu as pltpu

F32 = jnp.float32
BF16 = jnp.bfloat16
I32 = jnp.int32

D_MODEL = 1024
GRID_W = 64
HEAD_DIM = 64
N_HEADS_A = 8
N_KV_A = 2
REP_A = N_HEADS_A // N_KV_A
WINDOW = 128
ATTN_BLOCK = 128
N_HEADS_B = 4
RET_CHUNK = 256
N_HEADS_C = 4
N_BRANCH = 3
BRANCH_W = 512
D_FF = 2816
N_EXPERTS = 8
ROPE_BASE = 10000.0
ROPE_FREQS = HEAD_DIM // 4
NORM_EPS = 1e-6
NEG_INF = -1e30
LOG2E = math.log2(math.e)

LANES = 128
MOD_ROWS = 16
G_QA, G_KA, G_QB, G_KB, G_QD, G_KD = 0, 4, 6, 8, 10, 14
N_ROPE_GROUPS = 18
G_VA, G_VB, G_GB, G_VD, G_GL = 18, 20, 24, 28, 32
PROJ_W = 56 * LANES
M_SH1, M_SC1, M_G1, M_SH2, M_SC2, M_G2 = range(6)

FF_TILE = 1408
ROW_TILE = 512
PROJ_ROW_TILE = 1024
DIFF_Q_TILE = 1024
DIFF_K_TILE = 256
VMEM_LIMIT = 56 * 1024 * 1024


def _cparams(sem):
    return pltpu.CompilerParams(dimension_semantics=sem, vmem_limit_bytes=VMEM_LIMIT)


def _sigmoid(x):
    return 1.0 / (1.0 + jnp.exp(-x))


def _nt_dot(a, b):
    return lax.dot_general(a, b, (((1,), (1,)), ((), ())), preferred_element_type=F32)


def _head_slot(lane):
    return (lane // 32) % 2


def _rms_rows(x):
    return x * lax.rsqrt(jnp.mean(x * x, axis=-1, keepdims=True) + NORM_EPS)


def _mod_kernel(c_ref, w_ref, b_ref, o_ref):
    c = c_ref[...]
    s = c * _sigmoid(c)
    o_ref[0] = jnp.dot(s, w_ref[0], preferred_element_type=F32,
                       precision=lax.Precision.HIGHEST) + b_ref[0]


def _modulation(cc, w_mod, b_mod):
    depth, d, n = w_mod.shape
    tn = 1536
    return pl.pallas_call(
        _mod_kernel,
        grid=(depth, n // tn),
        in_specs=[pl.BlockSpec((MOD_ROWS, d), lambda l, j: (0, 0)),
                  pl.BlockSpec((1, d, tn), lambda l, j: (l, 0, j)),
                  pl.BlockSpec((1, 1, tn), lambda l, j: (l, 0, j))],
        out_specs=pl.BlockSpec((1, MOD_ROWS, tn), lambda l, j: (l, 0, j)),
        out_shape=jax.ShapeDtypeStruct((depth, MOD_ROWS, n), F32),
        compiler_params=_cparams(("arbitrary", "arbitrary")),
        name="modulation",
    )(cc, w_mod, b_mod.reshape(depth, 1, n))


def _mod_spec(chunk, nargs):
    if nargs == 1:
        return pl.BlockSpec((MOD_ROWS, D_MODEL), lambda i: (0, chunk))
    if nargs == 2:
        return pl.BlockSpec((MOD_ROWS, D_MODEL), lambda i, j: (0, chunk))
    return pl.BlockSpec((MOD_ROWS, D_MODEL), lambda i, j, *_: (0, chunk))


def _segment(i, n_lat_tiles, tiles_per_batch, nb):
    return jnp.where(i < n_lat_tiles, i // tiles_per_batch, nb)


def _proj_kernel(x_ref, gn_ref, sh_ref, sc_ref, w_ref, cos_ref, sin_ref, o_ref, h_ref, *,
                 n_lat_tiles, tiles_per_batch, nb, groups):
    i = pl.program_id(0)
    j = pl.program_id(1)

    @pl.when(j == 0)
    def _():
        seg = _segment(i, n_lat_tiles, tiles_per_batch, nb)
        sc = sc_ref[pl.ds(seg, 1), :]
        sh = sh_ref[pl.ds(seg, 1), :]
        h = (_rms_rows(x_ref[...]) * gn_ref[...]) * (1.0 + sc) + sh
        h_ref[...] = h.astype(BF16)

    acc = jnp.dot(h_ref[...], w_ref[...], preferred_element_type=F32)
    for g in range(groups):
        cols = slice(g * LANES, (g + 1) * LANES)
        a = acc[:, cols]
        y = a * cos_ref[...] + pltpu.roll(a, LANES // 2, 1) * sin_ref[...]
        o_ref[:, cols] = jnp.where(j * groups + g < N_ROPE_GROUPS, y, a).astype(BF16)


def _project(xs, gn, mod_l, w, tabs, *, tm, n_tiles, n_lat_tiles, tiles_per_batch, nb):
    tn = 1024
    groups = tn // LANES
    rope_map = lambda i, j: (jnp.where(i < n_lat_tiles, i % tiles_per_batch, tiles_per_batch), 0)
    kern = functools.partial(_proj_kernel, n_lat_tiles=n_lat_tiles, tiles_per_batch=tiles_per_batch,
                             nb=nb, groups=groups)
    return pl.pallas_call(
        kern,
        grid=(n_tiles, PROJ_W // tn),
        in_specs=[pl.BlockSpec((tm, D_MODEL), lambda i, j: (i, 0)),
                  pl.BlockSpec((1, D_MODEL), lambda i, j: (0, 0)),
                  _mod_spec(M_SH1, 2), _mod_spec(M_SC1, 2),
                  pl.BlockSpec((D_MODEL, tn), lambda i, j: (0, j)),
                  pl.BlockSpec((tm, LANES), rope_map),
                  pl.BlockSpec((tm, LANES), rope_map)],
        out_specs=pl.BlockSpec((tm, tn), lambda i, j: (i, j)),
        out_shape=jax.ShapeDtypeStruct((xs.shape[0], PROJ_W), BF16),
        scratch_shapes=[pltpu.VMEM((tm, D_MODEL), BF16)],
        compiler_params=_cparams(("arbitrary", "arbitrary")),
        name="in_proj",
    )(xs, gn, mod_l, mod_l, w, *tabs)


def _rope_tables(s, tm):
    rows = s // GRID_W
    row = jnp.repeat(jnp.arange(rows, dtype=F32), GRID_W)
    col = jnp.tile(jnp.arange(GRID_W, dtype=F32), rows)
    inv = 1.0 / (ROPE_BASE ** (jnp.arange(ROPE_FREQS, dtype=F32) / ROPE_FREQS))
    ar, ac = row[:, None] * inv, col[:, None] * inv
    cos = jnp.tile(jnp.concatenate([jnp.cos(ar), jnp.cos(ac)], axis=1), (1, 4))
    sin = jnp.tile(jnp.concatenate([jnp.sin(ar), jnp.sin(ac)], axis=1), (1, 2))
    sin = jnp.concatenate([-sin, sin], axis=1)
    ident = lambda t, v: jnp.concatenate([t, jnp.full((tm, LANES), v, F32)], axis=0)
    return ident(cos, 1.0), ident(sin, 0.0)


def _proj_weight(w):
    sl = lambda a, b: w[:, a:b]
    sc = HEAD_DIM ** -0.5
    dup = lambda a: jnp.concatenate([sl(a, a + 64), sl(a, a + 64), sl(a + 64, a + 128), sl(a + 64, a + 128)], axis=1)
    parts = [sl(0, 512) * (sc * LOG2E), dup(512), sl(768, 1024), sl(1024, 1280) * sc, sl(2304, 2816) * (sc * LOG2E),
             sl(2816, 3328), dup(640), sl(1280, 1792), sl(1792, 2304), sl(3328, 3840), sl(3840, 6912)]
    n_rope = N_ROPE_GROUPS * LANES
    rope = jnp.concatenate(parts[:6], axis=1).reshape(w.shape[0], N_ROPE_GROUPS, 2, 2, 2, ROPE_FREQS)
    rope = rope.transpose(0, 1, 4, 2, 3, 5).reshape(w.shape[0], n_rope)
    return jnp.concatenate([rope] + parts[6:], axis=1).astype(BF16)


def _gqa_kernel(*refs, local, s_len, tq):
    if local:
        (q_ref, kp_ref, kc_ref, kn_ref, kx_ref, vp_ref, vc_ref, vn_ref, vx_ref, sink_ref, _, o_ref) = refs
    else:
        q_ref, kx_ref, vx_ref, sink_ref, _, o_ref = refs
    lane = lax.broadcasted_iota(I32, (tq, LANES), 1)
    lo = lane < HEAD_DIM
    slot0 = _head_slot(lane) == 0
    if local:
        nk = 3 * tq + kx_ref.shape[0]
        qi = lax.broadcasted_iota(I32, (tq, nk), 0)
        kj = lax.broadcasted_iota(I32, (tq, nk), 1)
        rel = kj - qi
        kpos = pl.program_id(1) * tq - WINDOW + kj
        bad = ((rel < 0) | (rel > 2 * WINDOW) | (kpos < 0) | (kpos >= s_len)) & (kj < 3 * tq)
    for g in range(N_KV_A):
        cols = slice(g * LANES, (g + 1) * LANES)
        if local:
            kcat = jnp.concatenate([kp_ref[:, cols], kc_ref[:, cols], kn_ref[:, cols], kx_ref[:, cols]], axis=0)
            vcat = jnp.concatenate([vp_ref[:, cols], vc_ref[:, cols], vn_ref[:, cols], vx_ref[:, cols]], axis=0)
        else:
            kcat, vcat = kx_ref[:, cols], vx_ref[:, cols]
        qs = []
        for r in range(REP_A):
            h = g * REP_A + r
            qp = q_ref[:, (h // 2) * LANES:(h // 2 + 1) * LANES]
            qs.append(jnp.where(slot0 if h % 2 == 0 else jnp.logical_not(slot0), qp, jnp.zeros_like(qp)))
        s = _nt_dot(jnp.concatenate(qs, axis=0), kcat)
        ps, dens = [], []
        for r in range(REP_A):
            h = g * REP_A + r
            sr = s[r * tq:(r + 1) * tq]
            if local:
                sr = jnp.where(bad, NEG_INF, sr)
            sk = sink_ref[h:h + 1, 0:1] * LOG2E
            m = jnp.maximum(jnp.max(sr, axis=-1, keepdims=True), sk)
            p = jnp.exp2(sr - m)
            dens.append(jnp.sum(p, axis=-1, keepdims=True) + jnp.exp2(sk - m))
            ps.append(p.astype(BF16))
        o = jnp.dot(jnp.concatenate(ps, axis=0), vcat, preferred_element_type=F32)
        outs = [o[r * tq:(r + 1) * tq] / dens[r] for r in range(REP_A)]
        for pr in range(REP_A // 2):
            pair = jnp.where(lo, outs[2 * pr], outs[2 * pr + 1])
            c0 = (g * (REP_A // 2) + pr) * LANES
            o_ref[:, c0:c0 + LANES] = pair.astype(BF16)


def _gqa(proj, sink_tab, o_prev, *, nb, s_len, c_len, local):
    n_rows = proj.shape[0]
    cblk = nb * s_len // c_len
    kcol, vcol = G_KA // 2, G_VA // 2
    out_shape = jax.ShapeDtypeStruct((n_rows, N_HEADS_A * HEAD_DIM), BF16)
    if local:
        tq = ATTN_BLOCK
        nq = s_len // tq
        row = lambda b, i: b * nq + i
        specs = [pl.BlockSpec((tq, 4 * LANES), lambda b, i: (row(b, i), 0))]
        for col in (kcol, vcol):
            specs += [pl.BlockSpec((tq, 2 * LANES), lambda b, i, col=col: (row(b, jnp.maximum(i - 1, 0)), col)),
                      pl.BlockSpec((tq, 2 * LANES), lambda b, i, col=col: (row(b, i), col)),
                      pl.BlockSpec((tq, 2 * LANES), lambda b, i, col=col: (row(b, jnp.minimum(i + 1, nq - 1)), col)),
                      pl.BlockSpec((c_len, 2 * LANES), lambda b, i, col=col: (cblk + b, col))]
        specs += [pl.BlockSpec((N_HEADS_A, LANES), lambda b, i: (0, 0)),
                  pl.BlockSpec(memory_space=pl.ANY)]
        args = [proj] + [proj] * 8 + [sink_tab, o_prev]
        return pl.pallas_call(
            functools.partial(_gqa_kernel, local=True, s_len=s_len, tq=tq),
            grid=(nb, nq), in_specs=specs,
            out_specs=pl.BlockSpec((tq, 4 * LANES), lambda b, i: (row(b, i), 0)),
            out_shape=out_shape, input_output_aliases={10: 0},
            compiler_params=_cparams(("arbitrary", "arbitrary")), name="gqa_window",
        )(*args)
    specs = [pl.BlockSpec((c_len, 4 * LANES), lambda b: (cblk + b, 0)),
             pl.BlockSpec((c_len, 2 * LANES), lambda b: (cblk + b, kcol)),
             pl.BlockSpec((c_len, 2 * LANES), lambda b: (cblk + b, vcol)),
             pl.BlockSpec((N_HEADS_A, LANES), lambda b: (0, 0)),
             pl.BlockSpec(memory_space=pl.ANY)]
    return pl.pallas_call(
        functools.partial(_gqa_kernel, local=False, s_len=s_len, tq=c_len),
        grid=(nb,), in_specs=specs,
        out_specs=pl.BlockSpec((c_len, 4 * LANES), lambda b: (cblk + b, 0)),
        out_shape=out_shape, input_output_aliases={4: 0},
        compiler_params=_cparams(("arbitrary",)), name="gqa_context",
    )(proj, proj, proj, sink_tab, o_prev)


def _ret_kernel(q_ref, k_ref, v_ref, g_ref, dec_ref, sin_ref, _, o_ref, sout_ref, of_ref, ob_ref, *, n_rows):
    h = pl.program_id(1)
    c = RET_CHUNK
    nch = n_rows // c
    lane = lax.broadcasted_iota(I32, (c, LANES), 1)
    row = lax.broadcasted_iota(I32, (c, LANES), 0)
    qmask = _head_slot(lane) == (h % 2)

    def log_sigmoid(d):
        x = dec_ref[d, pl.ds(h, 1), :]
        return jnp.minimum(x, 0.0) - jnp.log(1.0 + jnp.exp(-jnp.abs(x)))

    lg_f, lg_b = log_sigmoid(0), log_sigmoid(1)
    diff = lax.broadcasted_iota(I32, (c, c), 0) - lax.broadcasted_iota(I32, (c, c), 1)
    rowf = row.astype(F32)
    dmat_f = jnp.where(diff >= 0, jnp.exp(jnp.maximum(diff, 0).astype(F32) * lg_f[:, 0:1]), 0.0)
    dmat_b = jnp.where(diff <= 0, jnp.exp(jnp.maximum(-diff, 0).astype(F32) * lg_b[:, 0:1]), 0.0)
    qdec_f = jnp.exp((rowf + 1.0) * lg_f)
    kdec_f = jnp.exp((c - 1.0 - rowf) * lg_f)
    qdec_b = jnp.exp((c - rowf) * lg_b)
    kdec_b = jnp.exp(rowf * lg_b)
    cdec_f = jnp.exp(c * lg_f)
    cdec_b = jnp.exp(c * lg_b)

    def chunk(n, state, dmat, qdec, kdec, cdec, dst_ref):
        rows = pl.ds(pl.multiple_of(n * c, c), c)
        q = q_ref[rows, :]
        q = jnp.where(qmask, q, jnp.zeros_like(q))
        k = k_ref[rows, :]
        v = v_ref[rows, :]
        sc = (_nt_dot(q, k) * dmat).astype(BF16)
        intra = jnp.dot(sc, v, preferred_element_type=F32)
        inter = jnp.dot((q.astype(F32) * qdec).astype(BF16), state.astype(BF16), preferred_element_type=F32)
        dst_ref[rows, :] = intra + inter
        kd = (k.astype(F32) * kdec).T.astype(BF16)
        return state * cdec + jnp.dot(kd, v, preferred_element_type=F32)

    def body(n, carry):
        sf, sb = carry
        sf = chunk(n, sf, dmat_f, qdec_f, kdec_f, cdec_f, of_ref)
        sb = chunk(nch - 1 - n, sb, dmat_b, qdec_b, kdec_b, cdec_b, ob_ref)
        return sf, sb

    sf, sb = lax.fori_loop(0, nch, body, (sin_ref[0, 0, 0], sin_ref[0, 0, 1]), unroll=2)
    sout_ref[0, 0, 0] = sf
    sout_ref[0, 0, 1] = sb
    o = of_ref[...] + ob_ref[...]
    gate = g_ref[...].astype(F32)
    o_ref[...] = (_rms_rows(o) * (gate * _sigmoid(gate))).astype(BF16)


def _retention(proj, dec_tab, s_in, o_prev, *, nb, s_len, c_len, latent):
    n_rows = s_len if latent else c_len
    base = 0 if latent else nb * s_len // c_len
    rb = lambda b: base + b
    col = lambda g0, per_pair: (lambda b, h: (rb(b), g0 + (h // 2 if per_pair else h)))
    state_spec = pl.BlockSpec((1, 1, 2, LANES, LANES), lambda b, h: (b, h, 0, 0, 0))
    specs = [pl.BlockSpec((n_rows, LANES), col(G_QB, True)),
             pl.BlockSpec((n_rows, LANES), col(G_KB, True)),
             pl.BlockSpec((n_rows, LANES), col(G_VB, False)),
             pl.BlockSpec((n_rows, LANES), col(G_GB, False)),
             pl.BlockSpec((2, N_HEADS_B, LANES), lambda b, h: (0, 0, 0)),
             state_spec, pl.BlockSpec(memory_space=pl.ANY)]
    args = [proj, proj, proj, proj, dec_tab, s_in, o_prev]
    aliases = {6: 0}
    return pl.pallas_call(
        functools.partial(_ret_kernel, n_rows=n_rows),
        grid=(nb, N_HEADS_B), in_specs=specs,
        out_specs=[pl.BlockSpec((n_rows, LANES), lambda b, h: (rb(b), h)), state_spec],
        out_shape=[jax.ShapeDtypeStruct((proj.shape[0], N_HEADS_B * LANES), BF16),
                   jax.ShapeDtypeStruct(s_in.shape, F32)],
        scratch_shapes=[pltpu.VMEM((n_rows, LANES), F32), pltpu.VMEM((n_rows, LANES), F32)],
        input_output_aliases=aliases,
        compiler_params=_cparams(("arbitrary", "arbitrary")),
        name="retention_latent" if latent else "retention_context",
    )(*args)


def _diff_kernel(*refs, latent, lam_init):
    if latent:
        q_ref, kl_ref, kx_ref, vl_ref, vx_ref, lam_ref, gain_ref, _, o_ref, k_scr, v_scr = refs
    else:
        q_ref, kx_ref, vx_ref, lam_ref, gain_ref, _, o_ref, k_scr, v_scr = refs
    nl = kl_ref.shape[0] if latent else 0

    def fill():
        if latent:
            k_scr[0:nl, :] = kl_ref[...]
            v_scr[0:nl, 0:LANES] = vl_ref[...]
        k_scr[nl:, :] = kx_ref[...]
        v_scr[nl:, 0:LANES] = vx_ref[...]
        v_scr[:, LANES:] = jnp.ones((v_scr.shape[0], LANES), BF16)

    if latent:
        pl.when(pl.program_id(2) == 0)(fill)
    else:
        fill()
    lp = lam_ref[...]
    lam = (jnp.exp(jnp.sum(lp[0:1] * lp[1:2], axis=-1, keepdims=True))
           - jnp.exp(jnp.sum(lp[2:3] * lp[3:4], axis=-1, keepdims=True)) + lam_init)
    q = q_ref[...]
    lo = _head_slot(lax.broadcasted_iota(I32, q.shape, 1)) == 0
    zero = jnp.zeros_like(q)
    n_keys = k_scr.shape[0]
    outs = []
    for comp in range(2):
        qc = jnp.where(lo, q, zero) if comp == 0 else jnp.where(lo, zero, q)
        m = acc = None
        for start in range(0, n_keys, DIFF_K_TILE):
            size = min(DIFF_K_TILE, n_keys - start)
            s = _nt_dot(qc, k_scr[start:start + size, :])
            m_t = jnp.max(s, axis=-1, keepdims=True)
            m_new = m_t if m is None else jnp.maximum(m, m_t)
            p = jnp.exp2(s - m_new).astype(BF16)
            pv = jnp.dot(p, v_scr[start:start + size, :], preferred_element_type=F32)
            acc = pv if acc is None else jnp.exp2(m - m_new) * acc + pv
            m = m_new
        outs.append(acc[:, :LANES] / acc[:, LANES:])
    o = outs[0] - lam * outs[1]
    o_ref[...] = (_rms_rows(o) * gain_ref[...] * (1.0 - lam_init)).astype(BF16)


def _diff_attention(proj, lam_p, gain, o_prev, *, nb, s_len, c_len, lam_init, latent):
    cblk = nb * s_len // c_len
    out_shape = jax.ShapeDtypeStruct((proj.shape[0], N_HEADS_C * LANES), BF16)
    small = [pl.BlockSpec((4, HEAD_DIM), lambda *_: (0, 0)), pl.BlockSpec((1, LANES), lambda *_: (0, 0))]
    if latent:
        tq = min(DIFF_Q_TILE, s_len)
        nq = s_len // tq
        specs = [pl.BlockSpec((tq, LANES), lambda b, h, i: (b * nq + i, G_QD + h)),
                 pl.BlockSpec((s_len, LANES), lambda b, h, i: (b, G_KD + h)),
                 pl.BlockSpec((c_len, LANES), lambda b, h, i: (cblk + b, G_KD + h)),
                 pl.BlockSpec((s_len, LANES), lambda b, h, i: (b, G_VD + h)),
                 pl.BlockSpec((c_len, LANES), lambda b, h, i: (cblk + b, G_VD + h))] + small
        specs.append(pl.BlockSpec(memory_space=pl.ANY))
        return pl.pallas_call(
            functools.partial(_diff_kernel, latent=True, lam_init=lam_init),
            grid=(nb, N_HEADS_C, nq), in_specs=specs,
            out_specs=pl.BlockSpec((tq, LANES), lambda b, h, i: (b * nq + i, h)),
            out_shape=out_shape, input_output_aliases={7: 0},
            scratch_shapes=[pltpu.VMEM((s_len + c_len, LANES), BF16), pltpu.VMEM((s_len + c_len, 2 * LANES), BF16)],
            compiler_params=_cparams(("arbitrary", "arbitrary", "arbitrary")), name="diff_latent",
        )(proj, proj, proj, proj, proj, lam_p, gain, o_prev)
    specs = [pl.BlockSpec((c_len, LANES), lambda b, h: (cblk + b, G_QD + h)),
             pl.BlockSpec((c_len, LANES), lambda b, h: (cblk + b, G_KD + h)),
             pl.BlockSpec((c_len, LANES), lambda b, h: (cblk + b, G_VD + h))] + small
    specs.append(pl.BlockSpec(memory_space=pl.ANY))
    return pl.pallas_call(
        functools.partial(_diff_kernel, latent=False, lam_init=lam_init),
        grid=(nb, N_HEADS_C), in_specs=specs,
        out_specs=pl.BlockSpec((c_len, LANES), lambda b, h: (cblk + b, h)),
        out_shape=out_shape, input_output_aliases={5: 0},
        scratch_shapes=[pltpu.VMEM((c_len, LANES), BF16), pltpu.VMEM((c_len, 2 * LANES), BF16)],
        compiler_params=_cparams(("arbitrary", "arbitrary")), name="diff_context",
    )(proj, proj, proj, lam_p, gain, o_prev)


def _merge_kernel(oa_ref, ob_ref, oc_ref, ga_ref, gb_ref, gc_ref, wb_ref, wo_ref, x_ref, g1_ref, sh_ref, sc_ref,
                  gn_ref, xo_ref, tok_ref, *, n_lat_tiles, tiles_per_batch, nb):
    i = pl.program_id(0)
    seg = _segment(i, n_lat_tiles, tiles_per_batch, nb)
    y = None
    for k, (o_ref, gl_ref) in enumerate(((oa_ref, ga_ref), (ob_ref, gb_ref), (oc_ref, gc_ref))):
        t = _sigmoid(gl_ref[...].astype(F32)) * jnp.dot(o_ref[...], wb_ref[k], preferred_element_type=F32)
        y = t if y is None else y + t
    z = jnp.dot(y.astype(BF16), wo_ref[...], preferred_element_type=F32)
    xn = x_ref[...] + g1_ref[pl.ds(seg, 1), :] * z
    xo_ref[...] = xn
    tok = (_rms_rows(xn) * gn_ref[...]) * (1.0 + sc_ref[pl.ds(seg, 1), :]) + sh_ref[pl.ds(seg, 1), :]
    tok_ref[...] = tok.astype(tok_ref.dtype)


def _merge(oa, ob, oc, proj, wb, wo, xs, mod_l, gn, *, tm, n_tiles, n_lat_tiles, tiles_per_batch, nb, tok_dtype):
    row = lambda w: pl.BlockSpec((tm, w), lambda i: (i, 0))
    gl = lambda k: pl.BlockSpec((tm, D_MODEL), lambda i: (i, G_GL // 8 + k))
    return pl.pallas_call(
        functools.partial(_merge_kernel, n_lat_tiles=n_lat_tiles, tiles_per_batch=tiles_per_batch, nb=nb),
        grid=(n_tiles,),
        in_specs=[row(BRANCH_W), row(BRANCH_W), row(BRANCH_W), gl(0), gl(1), gl(2),
                  pl.BlockSpec((N_BRANCH, BRANCH_W, D_MODEL), lambda i: (0, 0, 0)),
                  pl.BlockSpec((D_MODEL, D_MODEL), lambda i: (0, 0)),
                  row(D_MODEL), _mod_spec(M_G1, 1), _mod_spec(M_SH2, 1), _mod_spec(M_SC2, 1),
                  pl.BlockSpec((1, D_MODEL), lambda i: (0, 0))],
        out_specs=[row(D_MODEL), row(D_MODEL)],
        out_shape=[jax.ShapeDtypeStruct(xs.shape, F32), jax.ShapeDtypeStruct((n_tiles * tm, D_MODEL), tok_dtype)],
        input_output_aliases={8: 0},
        compiler_params=_cparams(("arbitrary",)), name="merge",
    )(oa, ob, oc, proj, proj, proj, wb, wo, xs, mod_l, mod_l, mod_l, gn)


def _ffn_kernel(be_ref, nu_ref, *refs, dense, n_lat_tiles, tiles_per_batch, nb):
    if dense:
        t_ref, w1_ref, w3_ref, w2_ref, x_ref, g2_ref, o_ref, acc_ref = refs
    else:
        t_ref, w1_ref, w3_ref, w2_ref, o_ref, acc_ref = refs
    i = pl.program_id(0)
    f = pl.program_id(1)
    last = pl.num_programs(1) - 1
    used = i < nu_ref[0]

    @pl.when(used)
    def _():
        t = t_ref[...].astype(BF16)
        h1 = jnp.dot(t, w1_ref[0], preferred_element_type=F32)
        h3 = jnp.dot(t, w3_ref[0], preferred_element_type=F32)
        a = ((h1 * _sigmoid(h1)) * h3).astype(BF16)
        part = jnp.dot(a, w2_ref[0], preferred_element_type=F32)

        @pl.when(f == 0)
        def _():
            acc_ref[...] = part

        @pl.when(f > 0)
        def _():
            acc_ref[...] += part

    @pl.when(f == last)
    def _():
        if dense:
            seg = _segment(i, n_lat_tiles, tiles_per_batch, nb)
            o_ref[...] = x_ref[...] + g2_ref[pl.ds(seg, 1), :] * acc_ref[...]
        else:
            o_ref[...] = jnp.where(used, acc_ref[...], 0.0)


def _ffn(tok, w1, w3, w2, block_expert, n_used, *, tm, n_blocks, xs=None, mod_l=None,
         n_lat_tiles=0, tiles_per_batch=1, nb=0):
    dense = xs is not None
    nf = D_FF // FF_TILE
    specs = [pl.BlockSpec((tm, D_MODEL), lambda i, f, be, nu: (i, 0)),
             pl.BlockSpec((1, D_MODEL, FF_TILE), lambda i, f, be, nu: (be[i], 0, f)),
             pl.BlockSpec((1, D_MODEL, FF_TILE), lambda i, f, be, nu: (be[i], 0, f)),
             pl.BlockSpec((1, FF_TILE, D_MODEL), lambda i, f, be, nu: (be[i], f, 0))]
    args = [tok, w1, w3, w2]
    aliases = {}
    if dense:
        specs += [pl.BlockSpec((tm, D_MODEL), lambda i, f, be, nu: (i, 0)), _mod_spec(M_G2, 4)]
        args += [xs, mod_l]
        aliases = {6: 0}
        out_rows = xs.shape[0]
    else:
        out_rows = tok.shape[0]
    return pl.pallas_call(
        functools.partial(_ffn_kernel, dense=dense, n_lat_tiles=n_lat_tiles, tiles_per_batch=tiles_per_batch, nb=nb),
        grid_spec=pltpu.PrefetchScalarGridSpec(
            num_scalar_prefetch=2, grid=(n_blocks, nf), in_specs=specs,
            out_specs=pl.BlockSpec((tm, D_MODEL), lambda i, f, be, nu: (i, 0)),
            scratch_shapes=[pltpu.VMEM((tm, D_MODEL), F32)]),
        out_shape=jax.ShapeDtypeStruct((out_rows, D_MODEL), F32),
        input_output_aliases=aliases,
        compiler_params=_cparams(("arbitrary", "arbitrary")),
        name="ffn_dense" if dense else "ffn_experts",
    )(block_expert, n_used, *args)


def _router_kernel(tok_ref, wr_ref, ids_ref, gate_ref, cnt_ref, base_ref, *, tr):
    i = pl.program_id(0)

    @pl.when(i == 0)
    def _():
        base_ref[...] = jnp.zeros_like(base_ref)

    logits = _nt_dot(wr_ref[...], tok_ref[...].astype(BF16))
    eidx = lax.broadcasted_iota(I32, logits.shape, 0)
    m1 = jnp.max(logits, axis=0, keepdims=True)
    i1 = jnp.min(jnp.where(logits == m1, eidx, N_EXPERTS), axis=0, keepdims=True)
    rest = jnp.where(eidx == i1, -jnp.inf, logits)
    m2 = jnp.max(rest, axis=0, keepdims=True)
    i2 = jnp.min(jnp.where(rest == m2, eidx, N_EXPERTS), axis=0, keepdims=True)
    e = jnp.exp(m2 - m1)
    sel1 = eidx == i1
    sel2 = eidx == i2
    both = jnp.where(sel1 | sel2, 1.0, 0.0)
    before = lax.broadcasted_iota(I32, (tr, tr), 0) < lax.broadcasted_iota(I32, (tr, tr), 1)
    prefix = jnp.dot(both.astype(BF16), jnp.where(before, 1.0, 0.0).astype(BF16), preferred_element_type=F32)
    pos = base_ref[:, 0:1] + prefix
    r1 = jnp.sum(jnp.where(sel1, pos, 0.0), axis=0, keepdims=True)
    r2 = jnp.sum(jnp.where(sel2, pos, 0.0), axis=0, keepdims=True)
    ids_ref[0:1, :] = i1
    ids_ref[1:2, :] = i2
    ids_ref[2:3, :] = r1.astype(I32)
    ids_ref[3:4, :] = r2.astype(I32)
    ids_ref[4:8, :] = jnp.zeros((4, tr), I32)
    gate_ref[0:1, :] = 1.0 / (1.0 + e)
    gate_ref[1:2, :] = e / (1.0 + e)
    gate_ref[2:8, :] = jnp.zeros((6, tr), F32)
    base_ref[...] = base_ref[...] + jnp.sum(both, axis=1, keepdims=True)
    cnt_ref[...] = base_ref[...].astype(I32)


def _router(tok, wr, *, n_rows, tr):
    return pl.pallas_call(
        functools.partial(_router_kernel, tr=tr),
        grid=(n_rows // tr,),
        in_specs=[pl.BlockSpec((tr, D_MODEL), lambda i: (i, 0)),
                  pl.BlockSpec((N_EXPERTS, D_MODEL), lambda i: (0, 0))],
        out_specs=[pl.BlockSpec((8, tr), lambda i: (0, i)), pl.BlockSpec((8, tr), lambda i: (0, i)),
                   pl.BlockSpec((N_EXPERTS, LANES), lambda i: (0, 0))],
        out_shape=[jax.ShapeDtypeStruct((8, n_rows), I32), jax.ShapeDtypeStruct((8, n_rows), F32),
                   jax.ShapeDtypeStruct((N_EXPERTS, LANES), I32)],
        scratch_shapes=[pltpu.VMEM((N_EXPERTS, LANES), F32)],
        compiler_params=_cparams(("arbitrary",)), name="router",
    )(tok, wr)


def _scatter_kernel(d0_ref, d1_ref, tok_ref, _, o_hbm, sem, *, tb):
    i = pl.program_id(0)

    def issue(r, carry):
        t = i * tb + r
        src = tok_ref.at[pl.ds(r, 1), :]
        pltpu.make_async_copy(src, o_hbm.at[pl.ds(d0_ref[t], 1), :], sem).start()
        pltpu.make_async_copy(src, o_hbm.at[pl.ds(d1_ref[t], 1), :], sem).start()
        return carry

    lax.fori_loop(0, tb, issue, 0, unroll=8)
    pltpu.make_async_copy(tok_ref, o_hbm.at[pl.ds(0, tb), :], sem).wait()
    pltpu.make_async_copy(tok_ref, o_hbm.at[pl.ds(0, tb), :], sem).wait()


def _dispatch(tok, dest0, dest1, *, n_rows, n_slots):
    tb = 2 * ROW_TILE if n_rows % (2 * ROW_TILE) == 0 else ROW_TILE
    return pl.pallas_call(
        functools.partial(_scatter_kernel, tb=tb),
        grid_spec=pltpu.PrefetchScalarGridSpec(
            num_scalar_prefetch=2, grid=(n_rows // tb,),
            in_specs=[pl.BlockSpec((tb, D_MODEL), lambda i, d0, d1: (i, 0)), pl.BlockSpec(memory_space=pl.ANY)],
            out_specs=pl.BlockSpec(memory_space=pl.ANY),
            scratch_shapes=[pltpu.SemaphoreType.DMA(())]),
        out_shape=jax.ShapeDtypeStruct((n_slots, D_MODEL), tok.dtype),
        input_output_aliases={3: 0},
        compiler_params=_cparams(("arbitrary",)), name="moe_dispatch",
    )(dest0, dest1, tok, jnp.zeros((n_slots, D_MODEL), tok.dtype))


def _combine_kernel(d0_ref, d1_ref, y_hbm, x_ref, gate_ref, g2_ref, gn_ref, o_ref, buf_ref, sem, *,
                    tc, n_lat_tiles, tiles_per_batch, nb, final):
    i = pl.program_id(0)

    def issue(r, carry):
        pltpu.make_async_copy(y_hbm.at[pl.ds(d0_ref[i * tc + r], 1), :], buf_ref.at[0, pl.ds(r, 1), :], sem).start()
        pltpu.make_async_copy(y_hbm.at[pl.ds(d1_ref[i * tc + r], 1), :], buf_ref.at[1, pl.ds(r, 1), :], sem).start()
        return carry

    lax.fori_loop(0, tc, issue, 0, unroll=8)
    pltpu.make_async_copy(y_hbm.at[pl.ds(0, tc), :], buf_ref.at[0], sem).wait()
    pltpu.make_async_copy(y_hbm.at[pl.ds(0, tc), :], buf_ref.at[1], sem).wait()
    seg = _segment(i, n_lat_tiles, tiles_per_batch, nb)
    gates = gate_ref[...]
    f = gates[:, 0:1] * buf_ref[0] + gates[:, 1:2] * buf_ref[1]
    xo = x_ref[...] + g2_ref[pl.ds(seg, 1), :] * f
    if final:
        xo = _rms_rows(xo) * gn_ref[...]
    o_ref[...] = xo


def _combine(dest0, dest1, y, xs, gates, mod_l, gn, *, tc, n_tiles, n_lat_tiles, tiles_per_batch, nb, final):
    out_rows = n_tiles * tc
    return pl.pallas_call(
        functools.partial(_combine_kernel, tc=tc, n_lat_tiles=n_lat_tiles, tiles_per_batch=tiles_per_batch,
                          nb=nb, final=final),
        grid_spec=pltpu.PrefetchScalarGridSpec(
            num_scalar_prefetch=2, grid=(n_tiles,),
            in_specs=[pl.BlockSpec(memory_space=pl.ANY),
                      pl.BlockSpec((tc, D_MODEL), lambda i, a, b: (i, 0)),
                      pl.BlockSpec((tc, 2), lambda i, a, b: (i, 0)),
                      _mod_spec(M_G2, 3),
                      pl.BlockSpec((1, D_MODEL), lambda i, a, b: (0, 0))],
            out_specs=pl.BlockSpec((tc, D_MODEL), lambda i, a, b: (i, 0)),
            scratch_shapes=[pltpu.VMEM((2, tc, D_MODEL), F32), pltpu.SemaphoreType.DMA(())]),
        out_shape=jax.ShapeDtypeStruct((out_rows, D_MODEL), F32),
        compiler_params=_cparams(("arbitrary",)), name="moe_combine",
    )(dest0, dest1, y, xs, gates, mod_l, gn)


def _moe(tok, xs, mod_l, w_router, w1, w3, w2, gn_final, *, n_rows, tm, n_lat_tiles, nb, s_len, final):
    bs = ROW_TILE
    ids, gates, counts = _router(tok, w_router.T.astype(BF16), n_rows=n_rows, tr=tm)
    counts = counts[:, 0]
    padded = (counts + bs - 1) // bs * bs
    pad_ends = jnp.cumsum(padded)
    starts = pad_ends - padded
    dest0 = starts[ids[0]] + ids[2]
    dest1 = starts[ids[1]] + ids[3]
    n_blocks = (2 * n_rows + bs - 1) // bs + N_EXPERTS
    block_start = jnp.arange(n_blocks, dtype=I32) * bs
    block_expert = jnp.minimum(jnp.sum((pad_ends[None, :] <= block_start[:, None]).astype(I32), axis=1),
                               N_EXPERTS - 1)
    n_used = (pad_ends[-1:] // bs).astype(I32)
    buf = _dispatch(tok, dest0, dest1, n_rows=n_rows, n_slots=n_blocks * bs)
    y = _ffn(buf, w1, w3, w2, block_expert, n_used, tm=bs, n_blocks=n_blocks)
    tc = ROW_TILE
    return _combine(dest0, dest1, y, xs, gates[0:2].T, mod_l, gn_final, tc=tc, n_tiles=n_rows // tc,
                    n_lat_tiles=n_lat_tiles * (tm // tc), tiles_per_batch=s_len // tc, nb=nb, final=final)


def kernel(x, c, ctx, c_ctx, w_mod, b_mod, norm_mix, norm_ffn, norm_final, w_in, attn_sink, ret_decay_fwd,
           ret_decay_bwd, diff_lambda, diff_norm, w_branch, w_out, ffn_w1, ffn_w3, ffn_w2, moe_router, moe_w1,
           moe_w3, moe_w2):
    nb, s_len, d = x.shape
    c_len = ctx.shape[1]
    depth = w_in.shape[0]
    assert d == D_MODEL and nb < MOD_ROWS and s_len % 256 == 0 and s_len % c_len == 0 and c_len % RET_CHUNK == 0
    n_lat, n_ctx = nb * s_len, nb * c_len
    tm = ROW_TILE
    assert s_len % tm == 0 and n_ctx % tm == 0
    n_lat_tiles, n_all_tiles, tiles_per_batch = n_lat // tm, (n_lat + n_ctx) // tm, s_len // tm
    tile_kw = dict(tm=tm, n_lat_tiles=n_lat_tiles, tiles_per_batch=tiles_per_batch, nb=nb)

    cc = jnp.zeros((MOD_ROWS, d), F32).at[:nb].set(c).at[nb].set(c_ctx)
    mod = _modulation(cc, w_mod, b_mod)
    ptm = PROJ_ROW_TILE if s_len % PROJ_ROW_TILE == 0 and n_ctx % PROJ_ROW_TILE == 0 else tm
    proj_kw = dict(tm=ptm, n_tiles=(n_lat + n_ctx) // ptm, n_lat_tiles=n_lat // ptm, tiles_per_batch=s_len // ptm,
                   nb=nb)
    tabs = _rope_tables(s_len, ptm)
    xs = jnp.concatenate([x.reshape(n_lat, d), ctx.reshape(n_ctx, d)], axis=0)
    one_block = jnp.zeros((n_all_tiles,), I32)
    row = lambda v: v.reshape(1, -1).astype(F32)

    for l in range(depth):
        ctx_out = l < depth - 1
        lam_init = 0.8 - 0.6 * math.exp(-0.3 * l)
        n_tiles = n_all_tiles if ctx_out else n_lat_tiles
        proj = _project(xs, row(norm_mix[l]), mod[l], _proj_weight(w_in[l]), tabs, **proj_kw)

        sink_tab = jnp.broadcast_to(attn_sink[l].astype(F32)[:, None], (N_HEADS_A, LANES))
        dec_tab = jnp.broadcast_to(jnp.stack([ret_decay_fwd[l], ret_decay_bwd[l]]).astype(F32)[:, :, None],
                                   (2, N_HEADS_B, LANES))
        lam_p, gain = diff_lambda[l].astype(F32), row(diff_norm[l])
        kw = dict(nb=nb, s_len=s_len, c_len=c_len)
        zero_state = jnp.zeros((nb, N_HEADS_B, 2, LANES, LANES), F32)
        oa, ob, oc = (jnp.zeros((n_lat + n_ctx, BRANCH_W), BF16) for _ in range(N_BRANCH))
        ob, state = _retention(proj, dec_tab, zero_state, ob, latent=False, **kw)
        ob, _ = _retention(proj, dec_tab, state, ob, latent=True, **kw)
        if ctx_out:
            oa = _gqa(proj, sink_tab, oa, local=False, **kw)
            oc = _diff_attention(proj, lam_p, gain, oc, lam_init=lam_init, latent=False, **kw)
        oa = _gqa(proj, sink_tab, oa, local=True, **kw)
        oc = _diff_attention(proj, lam_p, gain, oc, lam_init=lam_init, latent=True, **kw)

        is_moe = l % 2 == 1
        xs, tok = _merge(oa, ob, oc, proj, w_branch[l].astype(BF16), w_out[l].astype(BF16), xs, mod[l],
                         row(norm_ffn[l]), n_tiles=n_tiles, tok_dtype=F32 if is_moe else BF16, **tile_kw)
        final = l == depth - 1
        if is_moe:
            e = l // 2
            xs = _moe(tok, xs, mod[l], moe_router[e], moe_w1[e].astype(BF16), moe_w3[e].astype(BF16),
                      moe_w2[e].astype(BF16), row(norm_final), n_rows=n_tiles * tm, tm=tm,
                      n_lat_tiles=n_lat_tiles, nb=nb, s_len=s_len, final=final)
        else:
            e = l // 2
            xs = _ffn(tok, ffn_w1[e:e + 1].astype(BF16), ffn_w3[e:e + 1].astype(BF16), ffn_w2[e:e + 1].astype(BF16),
                      one_block, jnp.full((1,), n_tiles, I32), n_blocks=n_tiles, xs=xs, mod_l=mod[l], **tile_kw)
            assert not final
    return xs[:n_lat].reshape(nb, s_len, d)
```

```python
import functools
import math

import jax
import jax.numpy as jnp
from jax import lax
from jax.experimental import pallas as pl
from jax.experimental.pallas import tpu as pltpu

F32 = jnp.float32
BF16 = jnp.bfloat16
I32 = jnp.int32

D_MODEL = 1024
GRID_W = 64
HEAD_DIM = 64
N_HEADS_A = 8
N_KV_A = 2
REP_A = N_HEADS_A // N_KV_A
WINDOW = 128
ATTN_BLOCK = 128
N_HEADS_B = 4
RET_CHUNK = 256
N_HEADS_C = 4
N_BRANCH = 3
BRANCH_W = 512
D_FF = 2816
N_EXPERTS = 8
ROPE_BASE = 10000.0
ROPE_FREQS = HEAD_DIM // 4
NORM_EPS = 1e-6
NEG_INF = -1e30
LOG2E = math.log2(math.e)

LANES = 128
MOD_ROWS = 16
G_QA, G_KA, G_QB, G_KB, G_QD, G_KD = 0, 4, 6, 8, 10, 14
N_ROPE_GROUPS = 18
G_VA, G_VB, G_GB, G_VD, G_GL = 18, 20, 24, 28, 32
PROJ_W = 56 * LANES
M_SH1, M_SC1, M_G1, M_SH2, M_SC2, M_G2 = range(6)

FF_TILE = 1408
ROW_TILE = 512
PROJ_ROW_TILE = 1024
PROJ_COL_TILE = 1792
DIFF_Q_TILE = 1024
DIFF_K_TILE = 256
VMEM_LIMIT = 56 * 1024 * 1024


def _cparams(sem):
    return pltpu.CompilerParams(dimension_semantics=sem, vmem_limit_bytes=VMEM_LIMIT)


def _sigmoid(x):
    return 1.0 / (1.0 + jnp.exp(-x))


def _nt_dot(a, b):
    return lax.dot_general(a, b, (((1,), (1,)), ((), ())), preferred_element_type=F32)


def _head_slot(lane):
    return (lane // 32) % 2


def _rms_rows(x):
    return x * lax.rsqrt(jnp.mean(x * x, axis=-1, keepdims=True) + NORM_EPS)


def _mod_kernel(c_ref, w_ref, b_ref, o_ref):
    c = c_ref[...]
    s = c * _sigmoid(c)
    o_ref[0] = jnp.dot(s, w_ref[0], preferred_element_type=F32,
                       precision=lax.Precision.HIGHEST) + b_ref[0]


def _modulation(cc, w_mod, b_mod):
    depth, d, n = w_mod.shape
    tn = 1536
    return pl.pallas_call(
        _mod_kernel,
        grid=(depth, n // tn),
        in_specs=[pl.BlockSpec((MOD_ROWS, d), lambda l, j: (0, 0)),
                  pl.BlockSpec((1, d, tn), lambda l, j: (l, 0, j)),
                  pl.BlockSpec((1, 1, tn), lambda l, j: (l, 0, j))],
        out_specs=pl.BlockSpec((1, MOD_ROWS, tn), lambda l, j: (l, 0, j)),
        out_shape=jax.ShapeDtypeStruct((depth, MOD_ROWS, n), F32),
        compiler_params=_cparams(("arbitrary", "arbitrary")),
        name="modulation",
    )(cc, w_mod, b_mod.reshape(depth, 1, n))


def _mod_spec(chunk, nargs):
    if nargs == 1:
        return pl.BlockSpec((MOD_ROWS, D_MODEL), lambda i: (0, chunk))
    if nargs == 2:
        return pl.BlockSpec((MOD_ROWS, D_MODEL), lambda i, j: (0, chunk))
    return pl.BlockSpec((MOD_ROWS, D_MODEL), lambda i, j, *_: (0, chunk))


def _segment(i, n_lat_tiles, tiles_per_batch, nb):
    return jnp.where(i < n_lat_tiles, i // tiles_per_batch, nb)


def _proj_kernel(x_ref, gn_ref, sh_ref, sc_ref, w_ref, cos_ref, sin_ref, o_ref, h_ref, *,
                 n_lat_tiles, tiles_per_batch, nb, groups):
    i = pl.program_id(0)
    j = pl.program_id(1)

    @pl.when(j == 0)
    def _():
        seg = _segment(i, n_lat_tiles, tiles_per_batch, nb)
        sc = sc_ref[pl.ds(seg, 1), :]
        sh = sh_ref[pl.ds(seg, 1), :]
        h = (_rms_rows(x_ref[...]) * gn_ref[...]) * (1.0 + sc) + sh
        h_ref[...] = h.astype(BF16)

    acc = jnp.dot(h_ref[...], w_ref[...], preferred_element_type=F32)
    for g in range(groups):
        cols = slice(g * LANES, (g + 1) * LANES)
        a = acc[:, cols]
        y = a * cos_ref[...] + pltpu.roll(a, LANES // 2, 1) * sin_ref[...]
        o_ref[:, cols] = jnp.where(j * groups + g < N_ROPE_GROUPS, y, a).astype(BF16)


def _project(xs, gn, mod_l, w, tabs, *, tm, n_tiles, n_lat_tiles, tiles_per_batch, nb):
    tn = PROJ_COL_TILE
    groups = tn // LANES
    rope_map = lambda i, j: (jnp.where(i < n_lat_tiles, i % tiles_per_batch, tiles_per_batch), 0)
    kern = functools.partial(_proj_kernel, n_lat_tiles=n_lat_tiles, tiles_per_batch=tiles_per_batch,
                             nb=nb, groups=groups)
    return pl.pallas_call(
        kern,
        grid=(n_tiles, PROJ_W // tn),
        in_specs=[pl.BlockSpec((tm, D_MODEL), lambda i, j: (i, 0)),
                  pl.BlockSpec((1, D_MODEL), lambda i, j: (0, 0)),
                  _mod_spec(M_SH1, 2), _mod_spec(M_SC1, 2),
                  pl.BlockSpec((D_MODEL, tn), lambda i, j: (0, j)),
                  pl.BlockSpec((tm, LANES), rope_map),
                  pl.BlockSpec((tm, LANES), rope_map)],
        out_specs=pl.BlockSpec((tm, tn), lambda i, j: (i, j)),
        out_shape=jax.ShapeDtypeStruct((xs.shape[0], PROJ_W), BF16),
        scratch_shapes=[pltpu.VMEM((tm, D_MODEL), BF16)],
        compiler_params=_cparams(("arbitrary", "arbitrary")),
        name="in_proj",
    )(xs, gn, mod_l, mod_l, w, *tabs)


def _rope_tables(s, tm):
    rows = s // GRID_W
    row = jnp.repeat(jnp.arange(rows, dtype=F32), GRID_W)
    col = jnp.tile(jnp.arange(GRID_W, dtype=F32), rows)
    inv = 1.0 / (ROPE_BASE ** (jnp.arange(ROPE_FREQS, dtype=F32) / ROPE_FREQS))
    ar, ac = row[:, None] * inv, col[:, None] * inv
    cos = jnp.tile(jnp.concatenate([jnp.cos(ar), jnp.cos(ac)], axis=1), (1, 4))
    sin = jnp.tile(jnp.concatenate([jnp.sin(ar), jnp.sin(ac)], axis=1), (1, 2))
    sin = jnp.concatenate([-sin, sin], axis=1)
    ident = lambda t, v: jnp.concatenate([t, jnp.full((tm, LANES), v, F32)], axis=0)
    return ident(cos, 1.0), ident(sin, 0.0)


def _proj_weight(w):
    sl = lambda a, b: w[:, a:b]
    sc = HEAD_DIM ** -0.5
    dup = lambda a: jnp.concatenate([sl(a, a + 64), sl(a, a + 64), sl(a + 64, a + 128), sl(a + 64, a + 128)], axis=1)
    parts = [sl(0, 512) * (sc * LOG2E), dup(512), sl(768, 1024), sl(1024, 1280) * sc, sl(2304, 2816) * (sc * LOG2E),
             sl(2816, 3328), dup(640), sl(1280, 1792), sl(1792, 2304), sl(3328, 3840), sl(3840, 6912)]
    n_rope = N_ROPE_GROUPS * LANES
    rope = jnp.concatenate(parts[:6], axis=1).reshape(w.shape[0], N_ROPE_GROUPS, 2, 2, 2, ROPE_FREQS)
    rope = rope.transpose(0, 1, 4, 2, 3, 5).reshape(w.shape[0], n_rope)
    return jnp.concatenate([rope] + parts[6:], axis=1).astype(BF16)


def _gqa_kernel(*refs, local, s_len, tq):
    if local:
        (q_ref, kp_ref, kc_ref, kn_ref, kx_ref, vp_ref, vc_ref, vn_ref, vx_ref, sink_ref, _, o_ref) = refs
    else:
        q_ref, kx_ref, vx_ref, sink_ref, _, o_ref = refs
    lane = lax.broadcasted_iota(I32, (tq, LANES), 1)
    lo = lane < HEAD_DIM
    slot0 = _head_slot(lane) == 0
    if local:
        nk = 3 * tq + kx_ref.shape[0]
        qi = lax.broadcasted_iota(I32, (tq, nk), 0)
        kj = lax.broadcasted_iota(I32, (tq, nk), 1)
        rel = kj - qi
        kpos = pl.program_id(1) * tq - WINDOW + kj
        bad = ((rel < 0) | (rel > 2 * WINDOW) | (kpos < 0) | (kpos >= s_len)) & (kj < 3 * tq)
    for g in range(N_KV_A):
        cols = slice(g * LANES, (g + 1) * LANES)
        if local:
            kcat = jnp.concatenate([kp_ref[:, cols], kc_ref[:, cols], kn_ref[:, cols], kx_ref[:, cols]], axis=0)
            vcat = jnp.concatenate([vp_ref[:, cols], vc_ref[:, cols], vn_ref[:, cols], vx_ref[:, cols]], axis=0)
        else:
            kcat, vcat = kx_ref[:, cols], vx_ref[:, cols]
        qs = []
        for r in range(REP_A):
            h = g * REP_A + r
            qp = q_ref[:, (h // 2) * LANES:(h // 2 + 1) * LANES]
            qs.append(jnp.where(slot0 if h % 2 == 0 else jnp.logical_not(slot0), qp, jnp.zeros_like(qp)))
        s = _nt_dot(jnp.concatenate(qs, axis=0), kcat)
        ps, dens = [], []
        for r in range(REP_A):
            h = g * REP_A + r
            sr = s[r * tq:(r + 1) * tq]
            if local:
                sr = jnp.where(bad, NEG_INF, sr)
            sk = sink_ref[h:h + 1, 0:1] * LOG2E
            m = jnp.maximum(jnp.max(sr, axis=-1, keepdims=True), sk)
            p = jnp.exp2(sr - m)
            dens.append(jnp.sum(p, axis=-1, keepdims=True) + jnp.exp2(sk - m))
            ps.append(p.astype(BF16))
        o = jnp.dot(jnp.concatenate(ps, axis=0), vcat, preferred_element_type=F32)
        outs = [o[r * tq:(r + 1) * tq] / dens[r] for r in range(REP_A)]
        for pr in range(REP_A // 2):
            pair = jnp.where(lo, outs[2 * pr], outs[2 * pr + 1])
            c0 = (g * (REP_A // 2) + pr) * LANES
            o_ref[:, c0:c0 + LANES] = pair.astype(BF16)


def _gqa(proj, sink_tab, o_prev, *, nb, s_len, c_len, local):
    n_rows = proj.shape[0]
    cblk = nb * s_len // c_len
    kcol, vcol = G_KA // 2, G_VA // 2
    out_shape = jax.ShapeDtypeStruct((n_rows, N_HEADS_A * HEAD_DIM), BF16)
    if local:
        tq = ATTN_BLOCK
        nq = s_len // tq
        row = lambda b, i: b * nq + i
        specs = [pl.BlockSpec((tq, 4 * LANES), lambda b, i: (row(b, i), 0))]
        for col in (kcol, vcol):
            specs += [pl.BlockSpec((tq, 2 * LANES), lambda b, i, col=col: (row(b, jnp.maximum(i - 1, 0)), col)),
                      pl.BlockSpec((tq, 2 * LANES), lambda b, i, col=col: (row(b, i), col)),
                      pl.BlockSpec((tq, 2 * LANES), lambda b, i, col=col: (row(b, jnp.minimum(i + 1, nq - 1)), col)),
                      pl.BlockSpec((c_len, 2 * LANES), lambda b, i, col=col: (cblk + b, col))]
        specs += [pl.BlockSpec((N_HEADS_A, LANES), lambda b, i: (0, 0)),
                  pl.BlockSpec(memory_space=pl.ANY)]
        args = [proj] + [proj] * 8 + [sink_tab, o_prev]
        return pl.pallas_call(
            functools.partial(_gqa_kernel, local=True, s_len=s_len, tq=tq),
            grid=(nb, nq), in_specs=specs,
            out_specs=pl.BlockSpec((tq, 4 * LANES), lambda b, i: (row(b, i), 0)),
            out_shape=out_shape, input_output_aliases={10: 0},
            compiler_params=_cparams(("arbitrary", "arbitrary")), name="gqa_window",
        )(*args)
    specs = [pl.BlockSpec((c_len, 4 * LANES), lambda b: (cblk + b, 0)),
             pl.BlockSpec((c_len, 2 * LANES), lambda b: (cblk + b, kcol)),
             pl.BlockSpec((c_len, 2 * LANES), lambda b: (cblk + b, vcol)),
             pl.BlockSpec((N_HEADS_A, LANES), lambda b: (0, 0)),
             pl.BlockSpec(memory_space=pl.ANY)]
    return pl.pallas_call(
        functools.partial(_gqa_kernel, local=False, s_len=s_len, tq=c_len),
        grid=(nb,), in_specs=specs,
        out_specs=pl.BlockSpec((c_len, 4 * LANES), lambda b: (cblk + b, 0)),
        out_shape=out_shape, input_output_aliases={4: 0},
        compiler_params=_cparams(("arbitrary",)), name="gqa_context",
    )(proj, proj, proj, sink_tab, o_prev)


def _ret_kernel(q_ref, k_ref, v_ref, g_ref, dec_ref, sin_ref, _, o_ref, sout_ref, of_ref, ob_ref, *, n_rows):
    h = pl.program_id(1)
    c = min(RET_CHUNK, n_rows)
    nch = n_rows // c
    lane = lax.broadcasted_iota(I32, (c, LANES), 1)
    row = lax.broadcasted_iota(I32, (c, LANES), 0)
    qmask = _head_slot(lane) == (h % 2)

    def log_sigmoid(d):
        x = dec_ref[d, pl.ds(h, 1), :]
        return jnp.minimum(x, 0.0) - jnp.log(1.0 + jnp.exp(-jnp.abs(x)))

    lg_f, lg_b = log_sigmoid(0), log_sigmoid(1)
    diff = lax.broadcasted_iota(I32, (c, c), 0) - lax.broadcasted_iota(I32, (c, c), 1)
    rowf = row.astype(F32)
    dmat_f = jnp.where(diff >= 0, jnp.exp(jnp.maximum(diff, 0).astype(F32) * lg_f[:, 0:1]), 0.0)
    dmat_b = jnp.where(diff <= 0, jnp.exp(jnp.maximum(-diff, 0).astype(F32) * lg_b[:, 0:1]), 0.0)
    qdec_f = jnp.exp((rowf + 1.0) * lg_f)
    kdec_f = jnp.exp((c - 1.0 - rowf) * lg_f)
    qdec_b = jnp.exp((c - rowf) * lg_b)
    kdec_b = jnp.exp(rowf * lg_b)
    cdec_f = jnp.exp(c * lg_f)
    cdec_b = jnp.exp(c * lg_b)

    def chunk(n, state, dmat, qdec, kdec, cdec, dst_ref):
        rows = pl.ds(pl.multiple_of(n * c, c), c)
        q = q_ref[rows, :]
        q = jnp.where(qmask, q, jnp.zeros_like(q))
        k = k_ref[rows, :]
        v = v_ref[rows, :]
        sc = (_nt_dot(q, k) * dmat).astype(BF16)
        intra = jnp.dot(sc, v, preferred_element_type=F32)
        inter = jnp.dot((q.astype(F32) * qdec).astype(BF16), state.astype(BF16), preferred_element_type=F32)
        dst_ref[rows, :] = intra + inter
        kd = (k.astype(F32) * kdec).T.astype(BF16)
        return state * cdec + jnp.dot(kd, v, preferred_element_type=F32)

    def body(n, carry):
        sf, sb = carry
        sf = chunk(n, sf, dmat_f, qdec_f, kdec_f, cdec_f, of_ref)
        sb = chunk(nch - 1 - n, sb, dmat_b, qdec_b, kdec_b, cdec_b, ob_ref)
        return sf, sb

    sf, sb = lax.fori_loop(0, nch, body, (sin_ref[0, 0, 0], sin_ref[0, 0, 1]), unroll=2)
    sout_ref[0, 0, 0] = sf
    sout_ref[0, 0, 1] = sb
    o = of_ref[...] + ob_ref[...]
    gate = g_ref[...].astype(F32)
    o_ref[...] = (_rms_rows(o) * (gate * _sigmoid(gate))).astype(BF16)


def _retention(proj, dec_tab, s_in, o_prev, *, nb, s_len, c_len, latent):
    n_rows = s_len if latent else c_len
    base = 0 if latent else nb * s_len // c_len
    rb = lambda b: base + b
    col = lambda g0, per_pair: (lambda b, h: (rb(b), g0 + (h // 2 if per_pair else h)))
    state_spec = pl.BlockSpec((1, 1, 2, LANES, LANES), lambda b, h: (b, h, 0, 0, 0))
    specs = [pl.BlockSpec((n_rows, LANES), col(G_QB, True)),
             pl.BlockSpec((n_rows, LANES), col(G_KB, True)),
             pl.BlockSpec((n_rows, LANES), col(G_VB, False)),
             pl.BlockSpec((n_rows, LANES), col(G_GB, False)),
             pl.BlockSpec((2, N_HEADS_B, LANES), lambda b, h: (0, 0, 0)),
             state_spec, pl.BlockSpec(memory_space=pl.ANY)]
    args = [proj, proj, proj, proj, dec_tab, s_in, o_prev]
    aliases = {6: 0}
    return pl.pallas_call(
        functools.partial(_ret_kernel, n_rows=n_rows),
        grid=(nb, N_HEADS_B), in_specs=specs,
        out_specs=[pl.BlockSpec((n_rows, LANES), lambda b, h: (rb(b), h)), state_spec],
        out_shape=[jax.ShapeDtypeStruct((proj.shape[0], N_HEADS_B * LANES), BF16),
                   jax.ShapeDtypeStruct(s_in.shape, F32)],
        scratch_shapes=[pltpu.VMEM((n_rows, LANES), F32), pltpu.VMEM((n_rows, LANES), F32)],
        input_output_aliases=aliases,
        compiler_params=_cparams(("arbitrary", "arbitrary")),
        name="retention_latent" if latent else "retention_context",
    )(*args)


def _diff_kernel(*refs, latent, lam_init):
    if latent:
        q_ref, kl_ref, kx_ref, vl_ref, vx_ref, lam_ref, gain_ref, _, o_ref, k_scr, v_scr = refs
    else:
        q_ref, kx_ref, vx_ref, lam_ref, gain_ref, _, o_ref, k_scr, v_scr = refs
    nl = kl_ref.shape[0] if latent else 0

    def fill():
        if latent:
            k_scr[0:nl, :] = kl_ref[...]
            v_scr[0:nl, 0:LANES] = vl_ref[...]
        k_scr[nl:, :] = kx_ref[...]
        v_scr[nl:, 0:LANES] = vx_ref[...]
        v_scr[:, LANES:] = jnp.ones((v_scr.shape[0], LANES), BF16)

    if latent:
        pl.when(pl.program_id(2) == 0)(fill)
    else:
        fill()
    lp = lam_ref[...]
    lam = (jnp.exp(jnp.sum(lp[0:1] * lp[1:2], axis=-1, keepdims=True))
           - jnp.exp(jnp.sum(lp[2:3] * lp[3:4], axis=-1, keepdims=True)) + lam_init)
    q = q_ref[...]
    lo = _head_slot(lax.broadcasted_iota(I32, q.shape, 1)) == 0
    zero = jnp.zeros_like(q)
    n_keys = k_scr.shape[0]
    outs = []
    for comp in range(2):
        qc = jnp.where(lo, q, zero) if comp == 0 else jnp.where(lo, zero, q)
        m = acc = None
        for start in range(0, n_keys, DIFF_K_TILE):
            size = min(DIFF_K_TILE, n_keys - start)
            s = _nt_dot(qc, k_scr[start:start + size, :])
            m_t = jnp.max(s, axis=-1, keepdims=True)
            m_new = m_t if m is None else jnp.maximum(m, m_t)
            p = jnp.exp2(s - m_new).astype(BF16)
            pv = jnp.dot(p, v_scr[start:start + size, :], preferred_element_type=F32)
            acc = pv if acc is None else jnp.exp2(m - m_new) * acc + pv
            m = m_new
        outs.append(acc[:, :LANES] / acc[:, LANES:])
    o = outs[0] - lam * outs[1]
    o_ref[...] = (_rms_rows(o) * gain_ref[...] * (1.0 - lam_init)).astype(BF16)


def _diff_attention(proj, lam_p, gain, o_prev, *, nb, s_len, c_len, lam_init, latent):
    cblk = nb * s_len // c_len
    out_shape = jax.ShapeDtypeStruct((proj.shape[0], N_HEADS_C * LANES), BF16)
    small = [pl.BlockSpec((4, HEAD_DIM), lambda *_: (0, 0)), pl.BlockSpec((1, LANES), lambda *_: (0, 0))]
    if latent:
        tq = min(DIFF_Q_TILE, s_len)
        nq = s_len // tq
        specs = [pl.BlockSpec((tq, LANES), lambda b, h, i: (b * nq + i, G_QD + h)),
                 pl.BlockSpec((s_len, LANES), lambda b, h, i: (b, G_KD + h)),
                 pl.BlockSpec((c_len, LANES), lambda b, h, i: (cblk + b, G_KD + h)),
                 pl.BlockSpec((s_len, LANES), lambda b, h, i: (b, G_VD + h)),
                 pl.BlockSpec((c_len, LANES), lambda b, h, i: (cblk + b, G_VD + h))] + small
        specs.append(pl.BlockSpec(memory_space=pl.ANY))
        return pl.pallas_call(
            functools.partial(_diff_kernel, latent=True, lam_init=lam_init),
            grid=(nb, N_HEADS_C, nq), in_specs=specs,
            out_specs=pl.BlockSpec((tq, LANES), lambda b, h, i: (b * nq + i, h)),
            out_shape=out_shape, input_output_aliases={7: 0},
            scratch_shapes=[pltpu.VMEM((s_len + c_len, LANES), BF16), pltpu.VMEM((s_len + c_len, 2 * LANES), BF16)],
            compiler_params=_cparams(("arbitrary", "arbitrary", "arbitrary")), name="diff_latent",
        )(proj, proj, proj, proj, proj, lam_p, gain, o_prev)
    specs = [pl.BlockSpec((c_len, LANES), lambda b, h: (cblk + b, G_QD + h)),
             pl.BlockSpec((c_len, LANES), lambda b, h: (cblk + b, G_KD + h)),
             pl.BlockSpec((c_len, LANES), lambda b, h: (cblk + b, G_VD + h))] + small
    specs.append(pl.BlockSpec(memory_space=pl.ANY))
    return pl.pallas_call(
        functools.partial(_diff_kernel, latent=False, lam_init=lam_init),
        grid=(nb, N_HEADS_C), in_specs=specs,
        out_specs=pl.BlockSpec((c_len, LANES), lambda b, h: (cblk + b, h)),
        out_shape=out_shape, input_output_aliases={5: 0},
        scratch_shapes=[pltpu.VMEM((c_len, LANES), BF16), pltpu.VMEM((c_len, 2 * LANES), BF16)],
        compiler_params=_cparams(("arbitrary", "arbitrary")), name="diff_context",
    )(proj, proj, proj, lam_p, gain, o_prev)


def _merge_kernel(oa_ref, ob_ref, oc_ref, ga_ref, gb_ref, gc_ref, wb_ref, wo_ref, x_ref, g1_ref, sh_ref, sc_ref,
                  gn_ref, xo_ref, tok_ref, *, n_lat_tiles, tiles_per_batch, nb):
    i = pl.program_id(0)
    seg = _segment(i, n_lat_tiles, tiles_per_batch, nb)
    y = None
    for k, (o_ref, gl_ref) in enumerate(((oa_ref, ga_ref), (ob_ref, gb_ref), (oc_ref, gc_ref))):
        t = _sigmoid(gl_ref[...].astype(F32)) * jnp.dot(o_ref[...], wb_ref[k], preferred_element_type=F32)
        y = t if y is None else y + t
    z = jnp.dot(y.astype(BF16), wo_ref[...], preferred_element_type=F32)
    xn = x_ref[...] + g1_ref[pl.ds(seg, 1), :] * z
    xo_ref[...] = xn
    tok = (_rms_rows(xn) * gn_ref[...]) * (1.0 + sc_ref[pl.ds(seg, 1), :]) + sh_ref[pl.ds(seg, 1), :]
    tok_ref[...] = tok.astype(tok_ref.dtype)


def _merge(oa, ob, oc, proj, wb, wo, xs, mod_l, gn, *, tm, n_tiles, n_lat_tiles, tiles_per_batch, nb, tok_dtype):
    row = lambda w: pl.BlockSpec((tm, w), lambda i: (i, 0))
    gl = lambda k: pl.BlockSpec((tm, D_MODEL), lambda i: (i, G_GL // 8 + k))
    return pl.pallas_call(
        functools.partial(_merge_kernel, n_lat_tiles=n_lat_tiles, tiles_per_batch=tiles_per_batch, nb=nb),
        grid=(n_tiles,),
        in_specs=[row(BRANCH_W), row(BRANCH_W), row(BRANCH_W), gl(0), gl(1), gl(2),
                  pl.BlockSpec((N_BRANCH, BRANCH_W, D_MODEL), lambda i: (0, 0, 0)),
                  pl.BlockSpec((D_MODEL, D_MODEL), lambda i: (0, 0)),
                  row(D_MODEL), _mod_spec(M_G1, 1), _mod_spec(M_SH2, 1), _mod_spec(M_SC2, 1),
                  pl.BlockSpec((1, D_MODEL), lambda i: (0, 0))],
        out_specs=[row(D_MODEL), row(D_MODEL)],
        out_shape=[jax.ShapeDtypeStruct(xs.shape, F32), jax.ShapeDtypeStruct((n_tiles * tm, D_MODEL), tok_dtype)],
        input_output_aliases={8: 0},
        compiler_params=_cparams(("arbitrary",)), name="merge",
    )(oa, ob, oc, proj, proj, proj, wb, wo, xs, mod_l, mod_l, mod_l, gn)


def _ffn_kernel(be_ref, nu_ref, *refs, dense, n_lat_tiles, tiles_per_batch, nb):
    if dense:
        t_ref, w1_ref, w3_ref, w2_ref, x_ref, g2_ref, o_ref, acc_ref = refs
    else:
        t_ref, w1_ref, w3_ref, w2_ref, o_ref, acc_ref = refs
    i = pl.program_id(0)
    f = pl.program_id(1)
    last = pl.num_programs(1) - 1
    used = i < nu_ref[0]

    @pl.when(used)
    def _():
        t = t_ref[...].astype(BF16)
        h1 = jnp.dot(t, w1_ref[0], preferred_element_type=F32)
        h3 = jnp.dot(t, w3_ref[0], preferred_element_type=F32)
        a = ((h1 * _sigmoid(h1)) * h3).astype(BF16)
        part = jnp.dot(a, w2_ref[0], preferred_element_type=F32)

        @pl.when(f == 0)
        def _():
            acc_ref[...] = part

        @pl.when(f > 0)
        def _():
            acc_ref[...] += part

    @pl.when(f == last)
    def _():
        if dense:
            seg = _segment(i, n_lat_tiles, tiles_per_batch, nb)
            o_ref[...] = x_ref[...] + g2_ref[pl.ds(seg, 1), :] * acc_ref[...]
        else:
            o_ref[...] = jnp.where(used, acc_ref[...], 0.0)


def _ffn(tok, w1, w3, w2, block_expert, n_used, *, tm, n_blocks, xs=None, mod_l=None,
         n_lat_tiles=0, tiles_per_batch=1, nb=0):
    dense = xs is not None
    nf = D_FF // FF_TILE
    specs = [pl.BlockSpec((tm, D_MODEL), lambda i, f, be, nu: (i, 0)),
             pl.BlockSpec((1, D_MODEL, FF_TILE), lambda i, f, be, nu: (be[i], 0, f)),
             pl.BlockSpec((1, D_MODEL, FF_TILE), lambda i, f, be, nu: (be[i], 0, f)),
             pl.BlockSpec((1, FF_TILE, D_MODEL), lambda i, f, be, nu: (be[i], f, 0))]
    args = [tok, w1, w3, w2]
    aliases = {}
    if dense:
        specs += [pl.BlockSpec((tm, D_MODEL), lambda i, f, be, nu: (i, 0)), _mod_spec(M_G2, 4)]
        args += [xs, mod_l]
        aliases = {6: 0}
        out_rows = xs.shape[0]
    else:
        out_rows = tok.shape[0]
    return pl.pallas_call(
        functools.partial(_ffn_kernel, dense=dense, n_lat_tiles=n_lat_tiles, tiles_per_batch=tiles_per_batch, nb=nb),
        grid_spec=pltpu.PrefetchScalarGridSpec(
            num_scalar_prefetch=2, grid=(n_blocks, nf), in_specs=specs,
            out_specs=pl.BlockSpec((tm, D_MODEL), lambda i, f, be, nu: (i, 0)),
            scratch_shapes=[pltpu.VMEM((tm, D_MODEL), F32)]),
        out_shape=jax.ShapeDtypeStruct((out_rows, D_MODEL), F32),
        input_output_aliases=aliases,
        compiler_params=_cparams(("arbitrary", "arbitrary")),
        name="ffn_dense" if dense else "ffn_experts",
    )(block_expert, n_used, *args)


def _router_kernel(tok_ref, wr_ref, ids_ref, gate_ref, cnt_ref, base_ref, *, tr):
    i = pl.program_id(0)

    @pl.when(i == 0)
    def _():
        base_ref[...] = jnp.zeros_like(base_ref)

    logits = _nt_dot(wr_ref[...], tok_ref[...].astype(BF16))
    eidx = lax.broadcasted_iota(I32, logits.shape, 0)
    m1 = jnp.max(logits, axis=0, keepdims=True)
    i1 = jnp.min(jnp.where(logits == m1, eidx, N_EXPERTS), axis=0, keepdims=True)
    rest = jnp.where(eidx == i1, -jnp.inf, logits)
    m2 = jnp.max(rest, axis=0, keepdims=True)
    i2 = jnp.min(jnp.where(rest == m2, eidx, N_EXPERTS), axis=0, keepdims=True)
    e = jnp.exp(m2 - m1)
    sel1 = eidx == i1
    sel2 = eidx == i2
    both = jnp.where(sel1 | sel2, 1.0, 0.0)
    before = lax.broadcasted_iota(I32, (tr, tr), 0) < lax.broadcasted_iota(I32, (tr, tr), 1)
    prefix = jnp.dot(both.astype(BF16), jnp.where(before, 1.0, 0.0).astype(BF16), preferred_element_type=F32)
    pos = base_ref[:, 0:1] + prefix
    r1 = jnp.sum(jnp.where(sel1, pos, 0.0), axis=0, keepdims=True)
    r2 = jnp.sum(jnp.where(sel2, pos, 0.0), axis=0, keepdims=True)
    ids_ref[0:1, :] = i1
    ids_ref[1:2, :] = i2
    ids_ref[2:3, :] = r1.astype(I32)
    ids_ref[3:4, :] = r2.astype(I32)
    ids_ref[4:8, :] = jnp.zeros((4, tr), I32)
    gate_ref[0:1, :] = 1.0 / (1.0 + e)
    gate_ref[1:2, :] = e / (1.0 + e)
    gate_ref[2:8, :] = jnp.zeros((6, tr), F32)
    base_ref[...] = base_ref[...] + jnp.sum(both, axis=1, keepdims=True)
    cnt_ref[...] = base_ref[...].astype(I32)


def _router(tok, wr, *, n_rows, tr):
    return pl.pallas_call(
        functools.partial(_router_kernel, tr=tr),
        grid=(n_rows // tr,),
        in_specs=[pl.BlockSpec((tr, D_MODEL), lambda i: (i, 0)),
                  pl.BlockSpec((N_EXPERTS, D_MODEL), lambda i: (0, 0))],
        out_specs=[pl.BlockSpec((8, tr), lambda i: (0, i)), pl.BlockSpec((8, tr), lambda i: (0, i)),
                   pl.BlockSpec((N_EXPERTS, LANES), lambda i: (0, 0))],
        out_shape=[jax.ShapeDtypeStruct((8, n_rows), I32), jax.ShapeDtypeStruct((8, n_rows), F32),
                   jax.ShapeDtypeStruct((N_EXPERTS, LANES), I32)],
        scratch_shapes=[pltpu.VMEM((N_EXPERTS, LANES), F32)],
        compiler_params=_cparams(("arbitrary",)), name="router",
    )(tok, wr)


def _scatter_kernel(d0_ref, d1_ref, tok_ref, _, o_hbm, sem, *, tb):
    i = pl.program_id(0)

    def issue(r, carry):
        t = i * tb + r
        src = tok_ref.at[pl.ds(r, 1), :]
        pltpu.make_async_copy(src, o_hbm.at[pl.ds(d0_ref[t], 1), :], sem).start()
        pltpu.make_async_copy(src, o_hbm.at[pl.ds(d1_ref[t], 1), :], sem).start()
        return carry

    lax.fori_loop(0, tb, issue, 0, unroll=8)
    pltpu.make_async_copy(tok_ref, o_hbm.at[pl.ds(0, tb), :], sem).wait()
    pltpu.make_async_copy(tok_ref, o_hbm.at[pl.ds(0, tb), :], sem).wait()


def _dispatch(tok, dest0, dest1, *, n_rows, n_slots):
    tb = 2 * ROW_TILE if n_rows % (2 * ROW_TILE) == 0 else ROW_TILE
    return pl.pallas_call(
        functools.partial(_scatter_kernel, tb=tb),
        grid_spec=pltpu.PrefetchScalarGridSpec(
            num_scalar_prefetch=2, grid=(n_rows // tb,),
            in_specs=[pl.BlockSpec((tb, D_MODEL), lambda i, d0, d1: (i, 0)), pl.BlockSpec(memory_space=pl.ANY)],
            out_specs=pl.BlockSpec(memory_space=pl.ANY),
            scratch_shapes=[pltpu.SemaphoreType.DMA(())]),
        out_shape=jax.ShapeDtypeStruct((n_slots, D_MODEL), tok.dtype),
        input_output_aliases={3: 0},
        compiler_params=_cparams(("arbitrary",)), name="moe_dispatch",
    )(dest0, dest1, tok, jnp.zeros((n_slots, D_MODEL), tok.dtype))


def _combine_kernel(d0_ref, d1_ref, y_hbm, x_ref, gate_ref, g2_ref, gn_ref, o_ref, buf_ref, sem, *,
                    tc, n_lat_tiles, tiles_per_batch, nb, final):
    i = pl.program_id(0)

    def issue(r, carry):
        pltpu.make_async_copy(y_hbm.at[pl.ds(d0_ref[i * tc + r], 1), :], buf_ref.at[0, pl.ds(r, 1), :], sem).start()
        pltpu.make_async_copy(y_hbm.at[pl.ds(d1_ref[i * tc + r], 1), :], buf_ref.at[1, pl.ds(r, 1), :], sem).start()
        return carry

    lax.fori_loop(0, tc, issue, 0, unroll=8)
    pltpu.make_async_copy(y_hbm.at[pl.ds(0, tc), :], buf_ref.at[0], sem).wait()
    pltpu.make_async_copy(y_hbm.at[pl.ds(0, tc), :], buf_ref.at[1], sem).wait()
    seg = _segment(i, n_lat_tiles, tiles_per_batch, nb)
    gates = gate_ref[...]
    f = gates[:, 0:1] * buf_ref[0] + gates[:, 1:2] * buf_ref[1]
    xo = x_ref[...] + g2_ref[pl.ds(seg, 1), :] * f
    if final:
        xo = _rms_rows(xo) * gn_ref[...]
    o_ref[...] = xo


def _combine(dest0, dest1, y, xs, gates, mod_l, gn, *, tc, n_tiles, n_lat_tiles, tiles_per_batch, nb, final):
    out_rows = n_tiles * tc
    return pl.pallas_call(
        functools.partial(_combine_kernel, tc=tc, n_lat_tiles=n_lat_tiles, tiles_per_batch=tiles_per_batch,
                          nb=nb, final=final),
        grid_spec=pltpu.PrefetchScalarGridSpec(
            num_scalar_prefetch=2, grid=(n_tiles,),
            in_specs=[pl.BlockSpec(memory_space=pl.ANY),
                      pl.BlockSpec((tc, D_MODEL), lambda i, a, b: (i, 0)),
                      pl.BlockSpec((tc, 2), lambda i, a, b: (i, 0)),
                      _mod_spec(M_G2, 3),
                      pl.BlockSpec((1, D_MODEL), lambda i, a, b: (0, 0))],
            out_specs=pl.BlockSpec((tc, D_MODEL), lambda i, a, b: (i, 0)),
            scratch_shapes=[pltpu.VMEM((2, tc, D_MODEL), F32), pltpu.SemaphoreType.DMA(())]),
        out_shape=jax.ShapeDtypeStruct((out_rows, D_MODEL), F32),
        compiler_params=_cparams(("arbitrary",)), name="moe_combine",
    )(dest0, dest1, y, xs, gates, mod_l, gn)


def _moe(tok, xs, mod_l, w_router, w1, w3, w2, gn_final, *, n_rows, tm, n_lat_tiles, nb, s_len, final, first_expert):
    bs = ROW_TILE
    ids, gates, counts = _router(tok, w_router.T.astype(BF16), n_rows=n_rows, tr=tm)
    counts = counts[:, 0]
    padded = (counts + bs - 1) // bs * bs
    pad_ends = jnp.cumsum(padded)
    starts = pad_ends - padded
    dest0 = starts[ids[0]] + ids[2]
    dest1 = starts[ids[1]] + ids[3]
    n_blocks = (2 * n_rows + bs - 1) // bs + N_EXPERTS
    block_start = jnp.arange(n_blocks, dtype=I32) * bs
    block_expert = jnp.minimum(jnp.sum((pad_ends[None, :] <= block_start[:, None]).astype(I32), axis=1),
                               N_EXPERTS - 1) + first_expert
    n_used = (pad_ends[-1:] // bs).astype(I32)
    buf = _dispatch(tok, dest0, dest1, n_rows=n_rows, n_slots=n_blocks * bs)
    y = _ffn(buf, w1, w3, w2, block_expert, n_used, tm=bs, n_blocks=n_blocks)
    tc = ROW_TILE
    return _combine(dest0, dest1, y, xs, gates[0:2].T, mod_l, gn_final, tc=tc, n_tiles=n_rows // tc,
                    n_lat_tiles=n_lat_tiles * (tm // tc), tiles_per_batch=s_len // tc, nb=nb, final=final)


def kernel(x, c, ctx, c_ctx, w_mod, b_mod, norm_mix, norm_ffn, norm_final, w_in, attn_sink, ret_decay_fwd,
           ret_decay_bwd, diff_lambda, diff_norm, w_branch, w_out, ffn_w1, ffn_w3, ffn_w2, moe_router, moe_w1,
           moe_w3, moe_w2):
    nb, s_len, d = x.shape
    c_len = ctx.shape[1]
    depth = w_in.shape[0]
    assert d == D_MODEL and nb < MOD_ROWS and s_len % 256 == 0 and s_len % c_len == 0 and c_len % 128 == 0 and s_len % RET_CHUNK == 0
    n_lat, n_ctx = nb * s_len, nb * c_len
    tm = ROW_TILE
    assert s_len % tm == 0 and n_ctx % tm == 0
    n_lat_tiles, n_all_tiles, tiles_per_batch = n_lat // tm, (n_lat + n_ctx) // tm, s_len // tm
    tile_kw = dict(tm=tm, n_lat_tiles=n_lat_tiles, tiles_per_batch=tiles_per_batch, nb=nb)

    cc = jnp.zeros((MOD_ROWS, d), F32).at[:nb].set(c).at[nb].set(c_ctx)
    mod = _modulation(cc, w_mod, b_mod)
    ptm = PROJ_ROW_TILE if s_len % PROJ_ROW_TILE == 0 and n_ctx % PROJ_ROW_TILE == 0 else tm
    proj_kw = dict(tm=ptm, n_tiles=(n_lat + n_ctx) // ptm, n_lat_tiles=n_lat // ptm, tiles_per_batch=s_len // ptm,
                   nb=nb)
    tabs = _rope_tables(s_len, ptm)
    xs = jnp.concatenate([x.reshape(n_lat, d), ctx.reshape(n_ctx, d)], axis=0)
    one_block = jnp.zeros((n_all_tiles,), I32)
    row = lambda v: v.reshape(1, -1).astype(F32)
    ffn_w = [w.astype(BF16) for w in (ffn_w1, ffn_w3, ffn_w2)]
    moe_w = [w.astype(BF16).reshape((-1,) + w.shape[2:]) for w in (moe_w1, moe_w3, moe_w2)]

    for l in range(depth):
        ctx_out = l < depth - 1
        lam_init = 0.8 - 0.6 * math.exp(-0.3 * l)
        n_tiles = n_all_tiles if ctx_out else n_lat_tiles
        proj = _project(xs, row(norm_mix[l]), mod[l], _proj_weight(w_in[l]), tabs, **proj_kw)

        sink_tab = jnp.broadcast_to(attn_sink[l].astype(F32)[:, None], (N_HEADS_A, LANES))
        dec_tab = jnp.broadcast_to(jnp.stack([ret_decay_fwd[l], ret_decay_bwd[l]]).astype(F32)[:, :, None],
                                   (2, N_HEADS_B, LANES))
        lam_p, gain = diff_lambda[l].astype(F32), row(diff_norm[l])
        kw = dict(nb=nb, s_len=s_len, c_len=c_len)
        zero_state = jnp.zeros((nb, N_HEADS_B, 2, LANES, LANES), F32)
        oa, ob, oc = (jnp.zeros((n_lat + n_ctx, BRANCH_W), BF16) for _ in range(N_BRANCH))
        ob, state = _retention(proj, dec_tab, zero_state, ob, latent=False, **kw)
        ob, _ = _retention(proj, dec_tab, state, ob, latent=True, **kw)
        if ctx_out:
            oa = _gqa(proj, sink_tab, oa, local=False, **kw)
            oc = _diff_attention(proj, lam_p, gain, oc, lam_init=lam_init, latent=False, **kw)
        oa = _gqa(proj, sink_tab, oa, local=True, **kw)
        oc = _diff_attention(proj, lam_p, gain, oc, lam_init=lam_init, latent=True, **kw)

        is_moe = l % 2 == 1
        xs, tok = _merge(oa, ob, oc, proj, w_branch[l].astype(BF16), w_out[l].astype(BF16), xs, mod[l],
                         row(norm_ffn[l]), n_tiles=n_tiles, tok_dtype=F32 if is_moe else BF16, **tile_kw)
        final = l == depth - 1
        if is_moe:
            e = l // 2
            xs = _moe(tok, xs, mod[l], moe_router[e], *moe_w, row(norm_final), n_rows=n_tiles * tm, tm=tm,
                      n_lat_tiles=n_lat_tiles, nb=nb, s_len=s_len, final=final, first_expert=e * N_EXPERTS)
        else:
            e = l // 2
            xs = _ffn(tok, *ffn_w, one_block + e, jnp.full((1,), n_tiles, I32), n_blocks=n_tiles, xs=xs,
                      mod_l=mod[l], **tile_kw)
            assert not final
    return xs[:n_lat].reshape(nb, s_len, d)
```

```python
import functools
import math

import jax
import jax.numpy as jnp
from jax import lax
from jax.experimental import pallas as pl
from jax.experimental.pallas import tpu as pltpu

F32 = jnp.float32
BF16 = jnp.bfloat16
I32 = jnp.int32

D_MODEL = 1024
GRID_W = 64
HEAD_DIM = 64
N_HEADS_A = 8
N_KV_A = 2
REP_A = N_HEADS_A // N_KV_A
WINDOW = 128
ATTN_BLOCK = 128
N_HEADS_B = 4
RET_CHUNK = 256
N_HEADS_C = 4
N_BRANCH = 3
BRANCH_W = 512
D_FF = 2816
N_EXPERTS = 8
ROPE_BASE = 10000.0
ROPE_FREQS = HEAD_DIM // 4
NORM_EPS = 1e-6
NEG_INF = -1e30
LOG2E = math.log2(math.e)

LANES = 128
MOD_ROWS = 16
G_QA, G_KA, G_QB, G_KB, G_QD, G_KD = 0, 4, 6, 8, 10, 14
N_ROPE_GROUPS = 18
G_VA, G_VB, G_GB, G_VD, G_GL = 18, 20, 24, 28, 32
PROJ_W = 56 * LANES
M_SH1, M_SC1, M_G1, M_SH2, M_SC2, M_G2 = range(6)

FF_TILE = 1408
ROW_TILE = 512
PROJ_ROW_TILE = 1024
PROJ_COL_TILE = 1792
GQA_BLOCKS_PER_STEP = 4
DIFF_Q_TILE = 1024
DIFF_K_TILE = 256
VMEM_LIMIT = 56 * 1024 * 1024


def _cparams(sem):
    return pltpu.CompilerParams(dimension_semantics=sem, vmem_limit_bytes=VMEM_LIMIT)


def _sigmoid(x):
    return 1.0 / (1.0 + jnp.exp(-x))


def _nt_dot(a, b):
    return lax.dot_general(a, b, (((1,), (1,)), ((), ())), preferred_element_type=F32)


def _head_slot(lane):
    return (lane // 32) % 2


def _rms_rows(x):
    return x * lax.rsqrt(jnp.mean(x * x, axis=-1, keepdims=True) + NORM_EPS)


def _mod_kernel(c_ref, w_ref, b_ref, o_ref):
    c = c_ref[...]
    s = c * _sigmoid(c)
    o_ref[0] = jnp.dot(s, w_ref[0], preferred_element_type=F32,
                       precision=lax.Precision.HIGHEST) + b_ref[0]


def _modulation(cc, w_mod, b_mod):
    depth, d, n = w_mod.shape
    tn = 1536
    return pl.pallas_call(
        _mod_kernel,
        grid=(depth, n // tn),
        in_specs=[pl.BlockSpec((MOD_ROWS, d), lambda l, j: (0, 0)),
                  pl.BlockSpec((1, d, tn), lambda l, j: (l, 0, j)),
                  pl.BlockSpec((1, 1, tn), lambda l, j: (l, 0, j))],
        out_specs=pl.BlockSpec((1, MOD_ROWS, tn), lambda l, j: (l, 0, j)),
        out_shape=jax.ShapeDtypeStruct((depth, MOD_ROWS, n), F32),
        compiler_params=_cparams(("arbitrary", "arbitrary")),
        name="modulation",
    )(cc, w_mod, b_mod.reshape(depth, 1, n))


def _mod_spec(chunk, nargs):
    if nargs == 1:
        return pl.BlockSpec((MOD_ROWS, D_MODEL), lambda i: (0, chunk))
    if nargs == 2:
        return pl.BlockSpec((MOD_ROWS, D_MODEL), lambda i, j: (0, chunk))
    return pl.BlockSpec((MOD_ROWS, D_MODEL), lambda i, j, *_: (0, chunk))


def _segment(i, n_lat_tiles, tiles_per_batch, nb):
    return jnp.where(i < n_lat_tiles, i // tiles_per_batch, nb)


def _proj_kernel(x_ref, gn_ref, sh_ref, sc_ref, w_ref, cos_ref, sin_ref, o_ref, h_ref, *,
                 n_lat_tiles, tiles_per_batch, nb, groups):
    i = pl.program_id(0)
    j = pl.program_id(1)

    @pl.when(j == 0)
    def _():
        seg = _segment(i, n_lat_tiles, tiles_per_batch, nb)
        sc = sc_ref[pl.ds(seg, 1), :]
        sh = sh_ref[pl.ds(seg, 1), :]
        h = (_rms_rows(x_ref[...]) * gn_ref[...]) * (1.0 + sc) + sh
        h_ref[...] = h.astype(BF16)

    acc = jnp.dot(h_ref[...], w_ref[...], preferred_element_type=F32)
    for g in range(groups):
        cols = slice(g * LANES, (g + 1) * LANES)
        a = acc[:, cols]
        y = a * cos_ref[...] + pltpu.roll(a, LANES // 2, 1) * sin_ref[...]
        o_ref[:, cols] = jnp.where(j * groups + g < N_ROPE_GROUPS, y, a).astype(BF16)


def _project(xs, gn, mod_l, w, tabs, *, tm, n_tiles, n_lat_tiles, tiles_per_batch, nb):
    tn = PROJ_COL_TILE
    groups = tn // LANES
    rope_map = lambda i, j: (jnp.where(i < n_lat_tiles, i % tiles_per_batch, tiles_per_batch), 0)
    kern = functools.partial(_proj_kernel, n_lat_tiles=n_lat_tiles, tiles_per_batch=tiles_per_batch,
                             nb=nb, groups=groups)
    return pl.pallas_call(
        kern,
        grid=(n_tiles, PROJ_W // tn),
        in_specs=[pl.BlockSpec((tm, D_MODEL), lambda i, j: (i, 0)),
                  pl.BlockSpec((1, D_MODEL), lambda i, j: (0, 0)),
                  _mod_spec(M_SH1, 2), _mod_spec(M_SC1, 2),
                  pl.BlockSpec((D_MODEL, tn), lambda i, j: (0, j)),
                  pl.BlockSpec((tm, LANES), rope_map),
                  pl.BlockSpec((tm, LANES), rope_map)],
        out_specs=pl.BlockSpec((tm, tn), lambda i, j: (i, j)),
        out_shape=jax.ShapeDtypeStruct((xs.shape[0], PROJ_W), BF16),
        scratch_shapes=[pltpu.VMEM((tm, D_MODEL), BF16)],
        compiler_params=_cparams(("arbitrary", "arbitrary")),
        name="in_proj",
    )(xs, gn, mod_l, mod_l, w, *tabs)


def _rope_tables(s, tm):
    rows = s // GRID_W
    row = jnp.repeat(jnp.arange(rows, dtype=F32), GRID_W)
    col = jnp.tile(jnp.arange(GRID_W, dtype=F32), rows)
    inv = 1.0 / (ROPE_BASE ** (jnp.arange(ROPE_FREQS, dtype=F32) / ROPE_FREQS))
    ar, ac = row[:, None] * inv, col[:, None] * inv
    cos = jnp.tile(jnp.concatenate([jnp.cos(ar), jnp.cos(ac)], axis=1), (1, 4))
    sin = jnp.tile(jnp.concatenate([jnp.sin(ar), jnp.sin(ac)], axis=1), (1, 2))
    sin = jnp.concatenate([-sin, sin], axis=1)
    ident = lambda t, v: jnp.concatenate([t, jnp.full((tm, LANES), v, F32)], axis=0)
    return ident(cos, 1.0), ident(sin, 0.0)


def _proj_weight(w):
    sl = lambda a, b: w[:, a:b]
    sc = HEAD_DIM ** -0.5
    dup = lambda a: jnp.concatenate([sl(a, a + 64), sl(a, a + 64), sl(a + 64, a + 128), sl(a + 64, a + 128)], axis=1)
    parts = [sl(0, 512) * (sc * LOG2E), dup(512), sl(768, 1024), sl(1024, 1280) * sc, sl(2304, 2816) * (sc * LOG2E),
             sl(2816, 3328), dup(640), sl(1280, 1792), sl(1792, 2304), sl(3328, 3840), sl(3840, 6912)]
    n_rope = N_ROPE_GROUPS * LANES
    rope = jnp.concatenate(parts[:6], axis=1).reshape(w.shape[0], N_ROPE_GROUPS, 2, 2, 2, ROPE_FREQS)
    rope = rope.transpose(0, 1, 4, 2, 3, 5).reshape(w.shape[0], n_rope)
    return jnp.concatenate([rope] + parts[6:], axis=1).astype(BF16)


def _gqa_kernel(*refs, local, s_len, tq, n_qb):
    if local:
        q_ref = refs[0]
        k_refs, kx_ref = refs[1:n_qb + 3], refs[n_qb + 3]
        v_refs, vx_ref = refs[n_qb + 4:2 * n_qb + 6], refs[2 * n_qb + 6]
        sink_ref, _, o_ref = refs[2 * n_qb + 7:]
    else:
        q_ref, kx_ref, vx_ref, sink_ref, _, o_ref = refs
    lane = lax.broadcasted_iota(I32, (tq, LANES), 1)
    lo = lane < HEAD_DIM
    slot0 = _head_slot(lane) == 0
    if local:
        nk = 3 * tq + kx_ref.shape[0]
        qi = lax.broadcasted_iota(I32, (tq, nk), 0)
        kj = lax.broadcasted_iota(I32, (tq, nk), 1)
        rel = kj - qi
        off_band = (rel < 0) | (rel > 2 * WINDOW)
    for qb in range(n_qb):
        rows = slice(qb * tq, (qb + 1) * tq)
        if local:
            kpos = (pl.program_id(1) * n_qb + qb) * tq - WINDOW + kj
            bad = (off_band | (kpos < 0) | (kpos >= s_len)) & (kj < 3 * tq)
        for g in range(N_KV_A):
            cols = slice(g * LANES, (g + 1) * LANES)
            if local:
                kcat = jnp.concatenate([r[:, cols] for r in k_refs[qb:qb + 3]] + [kx_ref[:, cols]], axis=0)
                vcat = jnp.concatenate([r[:, cols] for r in v_refs[qb:qb + 3]] + [vx_ref[:, cols]], axis=0)
            else:
                kcat, vcat = kx_ref[:, cols], vx_ref[:, cols]
            qs = []
            for r in range(REP_A):
                h = g * REP_A + r
                qp = q_ref[rows, (h // 2) * LANES:(h // 2 + 1) * LANES]
                qs.append(jnp.where(slot0 if h % 2 == 0 else jnp.logical_not(slot0), qp, jnp.zeros_like(qp)))
            s = _nt_dot(jnp.concatenate(qs, axis=0), kcat)
            ps, dens = [], []
            for r in range(REP_A):
                h = g * REP_A + r
                sr = s[r * tq:(r + 1) * tq]
                if local:
                    sr = jnp.where(bad, NEG_INF, sr)
                sk = sink_ref[h:h + 1, 0:1] * LOG2E
                m = jnp.maximum(jnp.max(sr, axis=-1, keepdims=True), sk)
                p = jnp.exp2(sr - m)
                dens.append(jnp.sum(p, axis=-1, keepdims=True) + jnp.exp2(sk - m))
                ps.append(p.astype(BF16))
            o = jnp.dot(jnp.concatenate(ps, axis=0), vcat, preferred_element_type=F32)
            outs = [o[r * tq:(r + 1) * tq] / dens[r] for r in range(REP_A)]
            for pr in range(REP_A // 2):
                pair = jnp.where(lo, outs[2 * pr], outs[2 * pr + 1])
                c0 = (g * (REP_A // 2) + pr) * LANES
                o_ref[rows, c0:c0 + LANES] = pair.astype(BF16)


def _gqa(proj, sink_tab, o_prev, *, nb, s_len, c_len, local):
    n_rows = proj.shape[0]
    cblk = nb * s_len // c_len
    kcol, vcol = G_KA // 2, G_VA // 2
    out_shape = jax.ShapeDtypeStruct((n_rows, N_HEADS_A * HEAD_DIM), BF16)
    if local:
        tq = ATTN_BLOCK
        nq = s_len // tq
        n_qb = GQA_BLOCKS_PER_STEP if nq % GQA_BLOCKS_PER_STEP == 0 else 1
        specs = [pl.BlockSpec((n_qb * tq, 4 * LANES), lambda b, i: (b * (nq // n_qb) + i, 0))]
        for col in (kcol, vcol):
            specs += [pl.BlockSpec((tq, 2 * LANES),
                                   lambda b, i, col=col, j=j: (b * nq + jnp.clip(n_qb * i - 1 + j, 0, nq - 1), col))
                      for j in range(n_qb + 2)]
            specs.append(pl.BlockSpec((c_len, 2 * LANES), lambda b, i, col=col: (cblk + b, col)))
        specs += [pl.BlockSpec((N_HEADS_A, LANES), lambda b, i: (0, 0)),
                  pl.BlockSpec(memory_space=pl.ANY)]
        args = [proj] * (2 * n_qb + 7) + [sink_tab, o_prev]
        return pl.pallas_call(
            functools.partial(_gqa_kernel, local=True, s_len=s_len, tq=tq, n_qb=n_qb),
            grid=(nb, nq // n_qb), in_specs=specs,
            out_specs=pl.BlockSpec((n_qb * tq, 4 * LANES), lambda b, i: (b * (nq // n_qb) + i, 0)),
            out_shape=out_shape, input_output_aliases={2 * n_qb + 8: 0},
            compiler_params=_cparams(("arbitrary", "arbitrary")), name="gqa_window",
        )(*args)
    specs = [pl.BlockSpec((c_len, 4 * LANES), lambda b: (cblk + b, 0)),
             pl.BlockSpec((c_len, 2 * LANES), lambda b: (cblk + b, kcol)),
             pl.BlockSpec((c_len, 2 * LANES), lambda b: (cblk + b, vcol)),
             pl.BlockSpec((N_HEADS_A, LANES), lambda b: (0, 0)),
             pl.BlockSpec(memory_space=pl.ANY)]
    return pl.pallas_call(
        functools.partial(_gqa_kernel, local=False, s_len=s_len, tq=c_len, n_qb=1),
        grid=(nb,), in_specs=specs,
        out_specs=pl.BlockSpec((c_len, 4 * LANES), lambda b: (cblk + b, 0)),
        out_shape=out_shape, input_output_aliases={4: 0},
        compiler_params=_cparams(("arbitrary",)), name="gqa_context",
    )(proj, proj, proj, sink_tab, o_prev)


def _ret_kernel(q_ref, k_ref, v_ref, g_ref, dec_ref, sin_ref, _, o_ref, sout_ref, of_ref, ob_ref, *, n_rows):
    h = pl.program_id(1)
    c = min(RET_CHUNK, n_rows)
    nch = n_rows // c
    lane = lax.broadcasted_iota(I32, (c, LANES), 1)
    row = lax.broadcasted_iota(I32, (c, LANES), 0)
    qmask = _head_slot(lane) == (h % 2)

    def log_sigmoid(d):
        x = dec_ref[d, pl.ds(h, 1), :]
        return jnp.minimum(x, 0.0) - jnp.log(1.0 + jnp.exp(-jnp.abs(x)))

    lg_f, lg_b = log_sigmoid(0), log_sigmoid(1)
    diff = lax.broadcasted_iota(I32, (c, c), 0) - lax.broadcasted_iota(I32, (c, c), 1)
    rowf = row.astype(F32)
    dmat_f = jnp.where(diff >= 0, jnp.exp(jnp.maximum(diff, 0).astype(F32) * lg_f[:, 0:1]), 0.0)
    dmat_b = jnp.where(diff <= 0, jnp.exp(jnp.maximum(-diff, 0).astype(F32) * lg_b[:, 0:1]), 0.0)
    qdec_f = jnp.exp((rowf + 1.0) * lg_f)
    kdec_f = jnp.exp((c - 1.0 - rowf) * lg_f)
    qdec_b = jnp.exp((c - rowf) * lg_b)
    kdec_b = jnp.exp(rowf * lg_b)
    cdec_f = jnp.exp(c * lg_f)
    cdec_b = jnp.exp(c * lg_b)

    def chunk(n, state, dmat, qdec, kdec, cdec, dst_ref):
        rows = pl.ds(pl.multiple_of(n * c, c), c)
        q = q_ref[rows, :]
        q = jnp.where(qmask, q, jnp.zeros_like(q))
        k = k_ref[rows, :]
        v = v_ref[rows, :]
        sc = (_nt_dot(q, k) * dmat).astype(BF16)
        intra = jnp.dot(sc, v, preferred_element_type=F32)
        inter = jnp.dot((q.astype(F32) * qdec).astype(BF16), state.astype(BF16), preferred_element_type=F32)
        dst_ref[rows, :] = intra + inter
        kd = (k.astype(F32) * kdec).T.astype(BF16)
        return state * cdec + jnp.dot(kd, v, preferred_element_type=F32)

    def body(n, carry):
        sf, sb = carry
        sf = chunk(n, sf, dmat_f, qdec_f, kdec_f, cdec_f, of_ref)
        sb = chunk(nch - 1 - n, sb, dmat_b, qdec_b, kdec_b, cdec_b, ob_ref)
        return sf, sb

    sf, sb = lax.fori_loop(0, nch, body, (sin_ref[0, 0, 0], sin_ref[0, 0, 1]), unroll=2)
    sout_ref[0, 0, 0] = sf
    sout_ref[0, 0, 1] = sb
    o = of_ref[...] + ob_ref[...]
    gate = g_ref[...].astype(F32)
    o_ref[...] = (_rms_rows(o) * (gate * _sigmoid(gate))).astype(BF16)


def _retention(proj, dec_tab, s_in, o_prev, *, nb, s_len, c_len, latent):
    n_rows = s_len if latent else c_len
    base = 0 if latent else nb * s_len // c_len
    rb = lambda b: base + b
    col = lambda g0, per_pair: (lambda b, h: (rb(b), g0 + (h // 2 if per_pair else h)))
    state_spec = pl.BlockSpec((1, 1, 2, LANES, LANES), lambda b, h: (b, h, 0, 0, 0))
    specs = [pl.BlockSpec((n_rows, LANES), col(G_QB, True)),
             pl.BlockSpec((n_rows, LANES), col(G_KB, True)),
             pl.BlockSpec((n_rows, LANES), col(G_VB, False)),
             pl.BlockSpec((n_rows, LANES), col(G_GB, False)),
             pl.BlockSpec((2, N_HEADS_B, LANES), lambda b, h: (0, 0, 0)),
             state_spec, pl.BlockSpec(memory_space=pl.ANY)]
    args = [proj, proj, proj, proj, dec_tab, s_in, o_prev]
    aliases = {6: 0}
    return pl.pallas_call(
        functools.partial(_ret_kernel, n_rows=n_rows),
        grid=(nb, N_HEADS_B), in_specs=specs,
        out_specs=[pl.BlockSpec((n_rows, LANES), lambda b, h: (rb(b), h)), state_spec],
        out_shape=[jax.ShapeDtypeStruct((proj.shape[0], N_HEADS_B * LANES), BF16),
                   jax.ShapeDtypeStruct(s_in.shape, F32)],
        scratch_shapes=[pltpu.VMEM((n_rows, LANES), F32), pltpu.VMEM((n_rows, LANES), F32)],
        input_output_aliases=aliases,
        compiler_params=_cparams(("arbitrary", "arbitrary")),
        name="retention_latent" if latent else "retention_context",
    )(*args)


def _diff_kernel(*refs, latent, lam_init):
    if latent:
        q_ref, kl_ref, kx_ref, vl_ref, vx_ref, lam_ref, gain_ref, _, o_ref, k_scr, v_scr = refs
    else:
        q_ref, kx_ref, vx_ref, lam_ref, gain_ref, _, o_ref, k_scr, v_scr = refs
    nl = kl_ref.shape[0] if latent else 0

    def fill():
        if latent:
            k_scr[0:nl, :] = kl_ref[...]
            v_scr[0:nl, 0:LANES] = vl_ref[...]
        k_scr[nl:, :] = kx_ref[...]
        v_scr[nl:, 0:LANES] = vx_ref[...]
        v_scr[:, LANES:] = jnp.ones((v_scr.shape[0], LANES), BF16)

    if latent:
        pl.when(pl.program_id(2) == 0)(fill)
    else:
        fill()
    lp = lam_ref[...]
    lam = (jnp.exp(jnp.sum(lp[0:1] * lp[1:2], axis=-1, keepdims=True))
           - jnp.exp(jnp.sum(lp[2:3] * lp[3:4], axis=-1, keepdims=True)) + lam_init)
    q = q_ref[...]
    lo = _head_slot(lax.broadcasted_iota(I32, q.shape, 1)) == 0
    zero = jnp.zeros_like(q)
    n_keys = k_scr.shape[0]
    outs = []
    for comp in range(2):
        qc = jnp.where(lo, q, zero) if comp == 0 else jnp.where(lo, zero, q)
        m = acc = None
        for start in range(0, n_keys, DIFF_K_TILE):
            size = min(DIFF_K_TILE, n_keys - start)
            s = _nt_dot(qc, k_scr[start:start + size, :])
            m_t = jnp.max(s, axis=-1, keepdims=True)
            m_new = m_t if m is None else jnp.maximum(m, m_t)
            p = jnp.exp2(s - m_new).astype(BF16)
            pv = jnp.dot(p, v_scr[start:start + size, :], preferred_element_type=F32)
            acc = pv if acc is None else jnp.exp2(m - m_new) * acc + pv
            m = m_new
        outs.append(acc[:, :LANES] / acc[:, LANES:])
    o = outs[0] - lam * outs[1]
    o_ref[...] = (_rms_rows(o) * gain_ref[...] * (1.0 - lam_init)).astype(BF16)


def _diff_attention(proj, lam_p, gain, o_prev, *, nb, s_len, c_len, lam_init, latent):
    cblk = nb * s_len // c_len
    out_shape = jax.ShapeDtypeStruct((proj.shape[0], N_HEADS_C * LANES), BF16)
    small = [pl.BlockSpec((4, HEAD_DIM), lambda *_: (0, 0)), pl.BlockSpec((1, LANES), lambda *_: (0, 0))]
    if latent:
        tq = min(DIFF_Q_TILE, s_len)
        nq = s_len // tq
        specs = [pl.BlockSpec((tq, LANES), lambda b, h, i: (b * nq + i, G_QD + h)),
                 pl.BlockSpec((s_len, LANES), lambda b, h, i: (b, G_KD + h)),
                 pl.BlockSpec((c_len, LANES), lambda b, h, i: (cblk + b, G_KD + h)),
                 pl.BlockSpec((s_len, LANES), lambda b, h, i: (b, G_VD + h)),
                 pl.BlockSpec((c_len, LANES), lambda b, h, i: (cblk + b, G_VD + h))] + small
        specs.append(pl.BlockSpec(memory_space=pl.ANY))
        return pl.pallas_call(
            functools.partial(_diff_kernel, latent=True, lam_init=lam_init),
            grid=(nb, N_HEADS_C, nq), in_specs=specs,
            out_specs=pl.BlockSpec((tq, LANES), lambda b, h, i: (b * nq + i, h)),
            out_shape=out_shape, input_output_aliases={7: 0},
            scratch_shapes=[pltpu.VMEM((s_len + c_len, LANES), BF16), pltpu.VMEM((s_len + c_len, 2 * LANES), BF16)],
            compiler_params=_cparams(("arbitrary", "arbitrary", "arbitrary")), name="diff_latent",
        )(proj, proj, proj, proj, proj, lam_p, gain, o_prev)
    specs = [pl.BlockSpec((c_len, LANES), lambda b, h: (cblk + b, G_QD + h)),
             pl.BlockSpec((c_len, LANES), lambda b, h: (cblk + b, G_KD + h)),
             pl.BlockSpec((c_len, LANES), lambda b, h: (cblk + b, G_VD + h))] + small
    specs.append(pl.BlockSpec(memory_space=pl.ANY))
    return pl.pallas_call(
        functools.partial(_diff_kernel, latent=False, lam_init=lam_init),
        grid=(nb, N_HEADS_C), in_specs=specs,
        out_specs=pl.BlockSpec((c_len, LANES), lambda b, h: (cblk + b, h)),
        out_shape=out_shape, input_output_aliases={5: 0},
        scratch_shapes=[pltpu.VMEM((c_len, LANES), BF16), pltpu.VMEM((c_len, 2 * LANES), BF16)],
        compiler_params=_cparams(("arbitrary", "arbitrary")), name="diff_context",
    )(proj, proj, proj, lam_p, gain, o_prev)


def _merge_kernel(oa_ref, ob_ref, oc_ref, ga_ref, gb_ref, gc_ref, wb_ref, wo_ref, x_ref, g1_ref, sh_ref, sc_ref,
                  gn_ref, xo_ref, tok_ref, *, n_lat_tiles, tiles_per_batch, nb):
    i = pl.program_id(0)
    seg = _segment(i, n_lat_tiles, tiles_per_batch, nb)
    y = None
    for k, (o_ref, gl_ref) in enumerate(((oa_ref, ga_ref), (ob_ref, gb_ref), (oc_ref, gc_ref))):
        t = _sigmoid(gl_ref[...].astype(F32)) * jnp.dot(o_ref[...], wb_ref[k], preferred_element_type=F32)
        y = t if y is None else y + t
    z = jnp.dot(y.astype(BF16), wo_ref[...], preferred_element_type=F32)
    xn = x_ref[...] + g1_ref[pl.ds(seg, 1), :] * z
    xo_ref[...] = xn
    tok = (_rms_rows(xn) * gn_ref[...]) * (1.0 + sc_ref[pl.ds(seg, 1), :]) + sh_ref[pl.ds(seg, 1), :]
    tok_ref[...] = tok.astype(tok_ref.dtype)


def _merge(oa, ob, oc, proj, wb, wo, xs, mod_l, gn, *, tm, n_tiles, n_lat_tiles, tiles_per_batch, nb, tok_dtype):
    row = lambda w: pl.BlockSpec((tm, w), lambda i: (i, 0))
    gl = lambda k: pl.BlockSpec((tm, D_MODEL), lambda i: (i, G_GL // 8 + k))
    return pl.pallas_call(
        functools.partial(_merge_kernel, n_lat_tiles=n_lat_tiles, tiles_per_batch=tiles_per_batch, nb=nb),
        grid=(n_tiles,),
        in_specs=[row(BRANCH_W), row(BRANCH_W), row(BRANCH_W), gl(0), gl(1), gl(2),
                  pl.BlockSpec((N_BRANCH, BRANCH_W, D_MODEL), lambda i: (0, 0, 0)),
                  pl.BlockSpec((D_MODEL, D_MODEL), lambda i: (0, 0)),
                  row(D_MODEL), _mod_spec(M_G1, 1), _mod_spec(M_SH2, 1), _mod_spec(M_SC2, 1),
                  pl.BlockSpec((1, D_MODEL), lambda i: (0, 0))],
        out_specs=[row(D_MODEL), row(D_MODEL)],
        out_shape=[jax.ShapeDtypeStruct(xs.shape, F32), jax.ShapeDtypeStruct((n_tiles * tm, D_MODEL), tok_dtype)],
        input_output_aliases={8: 0},
        compiler_params=_cparams(("arbitrary",)), name="merge",
    )(oa, ob, oc, proj, proj, proj, wb, wo, xs, mod_l, mod_l, mod_l, gn)


def _ffn_kernel(be_ref, nu_ref, *refs, dense, n_lat_tiles, tiles_per_batch, nb):
    if dense:
        t_ref, w1_ref, w3_ref, w2_ref, x_ref, g2_ref, o_ref, acc_ref = refs
    else:
        t_ref, w1_ref, w3_ref, w2_ref, o_ref, acc_ref = refs
    i = pl.program_id(0)
    f = pl.program_id(1)
    last = pl.num_programs(1) - 1
    used = i < nu_ref[0]

    @pl.when(used)
    def _():
        t = t_ref[...].astype(BF16)
        h1 = jnp.dot(t, w1_ref[0], preferred_element_type=F32)
        h3 = jnp.dot(t, w3_ref[0], preferred_element_type=F32)
        a = ((h1 * _sigmoid(h1)) * h3).astype(BF16)
        part = jnp.dot(a, w2_ref[0], preferred_element_type=F32)

        @pl.when(f == 0)
        def _():
            acc_ref[...] = part

        @pl.when(f > 0)
        def _():
            acc_ref[...] += part

    @pl.when(f == last)
    def _():
        if dense:
            seg = _segment(i, n_lat_tiles, tiles_per_batch, nb)
            o_ref[...] = x_ref[...] + g2_ref[pl.ds(seg, 1), :] * acc_ref[...]
        else:
            o_ref[...] = jnp.where(used, acc_ref[...], 0.0)


def _ffn(tok, w1, w3, w2, block_expert, n_used, *, tm, n_blocks, xs=None, mod_l=None,
         n_lat_tiles=0, tiles_per_batch=1, nb=0):
    dense = xs is not None
    nf = D_FF // FF_TILE
    specs = [pl.BlockSpec((tm, D_MODEL), lambda i, f, be, nu: (i, 0)),
             pl.BlockSpec((1, D_MODEL, FF_TILE), lambda i, f, be, nu: (be[i], 0, f)),
             pl.BlockSpec((1, D_MODEL, FF_TILE), lambda i, f, be, nu: (be[i], 0, f)),
             pl.BlockSpec((1, FF_TILE, D_MODEL), lambda i, f, be, nu: (be[i], f, 0))]
    args = [tok, w1, w3, w2]
    aliases = {}
    if dense:
        specs += [pl.BlockSpec((tm, D_MODEL), lambda i, f, be, nu: (i, 0)), _mod_spec(M_G2, 4)]
        args += [xs, mod_l]
        aliases = {6: 0}
        out_rows = xs.shape[0]
    else:
        out_rows = tok.shape[0]
    return pl.pallas_call(
        functools.partial(_ffn_kernel, dense=dense, n_lat_tiles=n_lat_tiles, tiles_per_batch=tiles_per_batch, nb=nb),
        grid_spec=pltpu.PrefetchScalarGridSpec(
            num_scalar_prefetch=2, grid=(n_blocks, nf), in_specs=specs,
            out_specs=pl.BlockSpec((tm, D_MODEL), lambda i, f, be, nu: (i, 0)),
            scratch_shapes=[pltpu.VMEM((tm, D_MODEL), F32)]),
        out_shape=jax.ShapeDtypeStruct((out_rows, D_MODEL), F32),
        input_output_aliases=aliases,
        compiler_params=_cparams(("arbitrary", "arbitrary")),
        name="ffn_dense" if dense else "ffn_experts",
    )(block_expert, n_used, *args)


def _router_kernel(tok_ref, wr_ref, ids_ref, gate_ref, cnt_ref, base_ref, *, tr):
    i = pl.program_id(0)

    @pl.when(i == 0)
    def _():
        base_ref[...] = jnp.zeros_like(base_ref)

    logits = _nt_dot(wr_ref[...], tok_ref[...].astype(BF16))
    eidx = lax.broadcasted_iota(I32, logits.shape, 0)
    m1 = jnp.max(logits, axis=0, keepdims=True)
    i1 = jnp.min(jnp.where(logits == m1, eidx, N_EXPERTS), axis=0, keepdims=True)
    rest = jnp.where(eidx == i1, -jnp.inf, logits)
    m2 = jnp.max(rest, axis=0, keepdims=True)
    i2 = jnp.min(jnp.where(rest == m2, eidx, N_EXPERTS), axis=0, keepdims=True)
    e = jnp.exp(m2 - m1)
    sel1 = eidx == i1
    sel2 = eidx == i2
    both = jnp.where(sel1 | sel2, 1.0, 0.0)
    before = lax.broadcasted_iota(I32, (tr, tr), 0) < lax.broadcasted_iota(I32, (tr, tr), 1)
    prefix = jnp.dot(both.astype(BF16), jnp.where(before, 1.0, 0.0).astype(BF16), preferred_element_type=F32)
    pos = base_ref[:, 0:1] + prefix
    r1 = jnp.sum(jnp.where(sel1, pos, 0.0), axis=0, keepdims=True)
    r2 = jnp.sum(jnp.where(sel2, pos, 0.0), axis=0, keepdims=True)
    ids_ref[0:1, :] = i1
    ids_ref[1:2, :] = i2
    ids_ref[2:3, :] = r1.astype(I32)
    ids_ref[3:4, :] = r2.astype(I32)
    ids_ref[4:8, :] = jnp.zeros((4, tr), I32)
    gate_ref[0:1, :] = 1.0 / (1.0 + e)
    gate_ref[1:2, :] = e / (1.0 + e)
    gate_ref[2:8, :] = jnp.zeros((6, tr), F32)
    base_ref[...] = base_ref[...] + jnp.sum(both, axis=1, keepdims=True)
    cnt_ref[...] = base_ref[...].astype(I32)


def _router(tok, wr, *, n_rows, tr):
    return pl.pallas_call(
        functools.partial(_router_kernel, tr=tr),
        grid=(n_rows // tr,),
        in_specs=[pl.BlockSpec((tr, D_MODEL), lambda i: (i, 0)),
                  pl.BlockSpec((N_EXPERTS, D_MODEL), lambda i: (0, 0))],
        out_specs=[pl.BlockSpec((8, tr), lambda i: (0, i)), pl.BlockSpec((8, tr), lambda i: (0, i)),
                   pl.BlockSpec((N_EXPERTS, LANES), lambda i: (0, 0))],
        out_shape=[jax.ShapeDtypeStruct((8, n_rows), I32), jax.ShapeDtypeStruct((8, n_rows), F32),
                   jax.ShapeDtypeStruct((N_EXPERTS, LANES), I32)],
        scratch_shapes=[pltpu.VMEM((N_EXPERTS, LANES), F32)],
        compiler_params=_cparams(("arbitrary",)), name="router",
    )(tok, wr)


def _scatter_kernel(d0_ref, d1_ref, tok_ref, _, o_hbm, sem, *, tb):
    i = pl.program_id(0)

    def issue(r, carry):
        t = i * tb + r
        src = tok_ref.at[pl.ds(r, 1), :]
        pltpu.make_async_copy(src, o_hbm.at[pl.ds(d0_ref[t], 1), :], sem).start()
        pltpu.make_async_copy(src, o_hbm.at[pl.ds(d1_ref[t], 1), :], sem).start()
        return carry

    lax.fori_loop(0, tb, issue, 0, unroll=8)
    pltpu.make_async_copy(tok_ref, o_hbm.at[pl.ds(0, tb), :], sem).wait()
    pltpu.make_async_copy(tok_ref, o_hbm.at[pl.ds(0, tb), :], sem).wait()


def _dispatch(tok, dest0, dest1, *, n_rows, n_slots):
    tb = 2 * ROW_TILE if n_rows % (2 * ROW_TILE) == 0 else ROW_TILE
    return pl.pallas_call(
        functools.partial(_scatter_kernel, tb=tb),
        grid_spec=pltpu.PrefetchScalarGridSpec(
            num_scalar_prefetch=2, grid=(n_rows // tb,),
            in_specs=[pl.BlockSpec((tb, D_MODEL), lambda i, d0, d1: (i, 0)), pl.BlockSpec(memory_space=pl.ANY)],
            out_specs=pl.BlockSpec(memory_space=pl.ANY),
            scratch_shapes=[pltpu.SemaphoreType.DMA(())]),
        out_shape=jax.ShapeDtypeStruct((n_slots, D_MODEL), tok.dtype),
        input_output_aliases={3: 0},
        compiler_params=_cparams(("arbitrary",)), name="moe_dispatch",
    )(dest0, dest1, tok, jnp.zeros((n_slots, D_MODEL), tok.dtype))


def _combine_kernel(d0_ref, d1_ref, y_hbm, x_ref, gate_ref, g2_ref, gn_ref, o_ref, buf_ref, sem, *,
                    tc, n_lat_tiles, tiles_per_batch, nb, final):
    i = pl.program_id(0)

    def issue(r, carry):
        pltpu.make_async_copy(y_hbm.at[pl.ds(d0_ref[i * tc + r], 1), :], buf_ref.at[0, pl.ds(r, 1), :], sem).start()
        pltpu.make_async_copy(y_hbm.at[pl.ds(d1_ref[i * tc + r], 1), :], buf_ref.at[1, pl.ds(r, 1), :], sem).start()
        return carry

    lax.fori_loop(0, tc, issue, 0, unroll=8)
    pltpu.make_async_copy(y_hbm.at[pl.ds(0, tc), :], buf_ref.at[0], sem).wait()
    pltpu.make_async_copy(y_hbm.at[pl.ds(0, tc), :], buf_ref.at[1], sem).wait()
    seg = _segment(i, n_lat_tiles, tiles_per_batch, nb)
    gates = gate_ref[...]
    f = gates[:, 0:1] * buf_ref[0] + gates[:, 1:2] * buf_ref[1]
    xo = x_ref[...] + g2_ref[pl.ds(seg, 1), :] * f
    if final:
        xo = _rms_rows(xo) * gn_ref[...]
    o_ref[...] = xo


def _combine(dest0, dest1, y, xs, gates, mod_l, gn, *, tc, n_tiles, n_lat_tiles, tiles_per_batch, nb, final):
    out_rows = n_tiles * tc
    return pl.pallas_call(
        functools.partial(_combine_kernel, tc=tc, n_lat_tiles=n_lat_tiles, tiles_per_batch=tiles_per_batch,
                          nb=nb, final=final),
        grid_spec=pltpu.PrefetchScalarGridSpec(
            num_scalar_prefetch=2, grid=(n_tiles,),
            in_specs=[pl.BlockSpec(memory_space=pl.ANY),
                      pl.BlockSpec((tc, D_MODEL), lambda i, a, b: (i, 0)),
                      pl.BlockSpec((tc, 2), lambda i, a, b: (i, 0)),
                      _mod_spec(M_G2, 3),
                      pl.BlockSpec((1, D_MODEL), lambda i, a, b: (0, 0))],
            out_specs=pl.BlockSpec((tc, D_MODEL), lambda i, a, b: (i, 0)),
            scratch_shapes=[pltpu.VMEM((2, tc, D_MODEL), F32), pltpu.SemaphoreType.DMA(())]),
        out_shape=jax.ShapeDtypeStruct((out_rows, D_MODEL), F32),
        compiler_params=_cparams(("arbitrary",)), name="moe_combine",
    )(dest0, dest1, y, xs, gates, mod_l, gn)


def _moe(tok, xs, mod_l, w_router, w1, w3, w2, gn_final, *, n_rows, tm, n_lat_tiles, nb, s_len, final, first_expert):
    bs = ROW_TILE
    ids, gates, counts = _router(tok, w_router.T.astype(BF16), n_rows=n_rows, tr=tm)
    counts = counts[:, 0]
    padded = (counts + bs - 1) // bs * bs
    pad_ends = jnp.cumsum(padded)
    starts = pad_ends - padded
    dest0 = starts[ids[0]] + ids[2]
    dest1 = starts[ids[1]] + ids[3]
    n_blocks = (2 * n_rows + bs - 1) // bs + N_EXPERTS
    block_start = jnp.arange(n_blocks, dtype=I32) * bs
    block_expert = jnp.minimum(jnp.sum((pad_ends[None, :] <= block_start[:, None]).astype(I32), axis=1),
                               N_EXPERTS - 1) + first_expert
    n_used = (pad_ends[-1:] // bs).astype(I32)
    buf = _dispatch(tok, dest0, dest1, n_rows=n_rows, n_slots=n_blocks * bs)
    y = _ffn(buf, w1, w3, w2, block_expert, n_used, tm=bs, n_blocks=n_blocks)
    tc = ROW_TILE
    return _combine(dest0, dest1, y, xs, gates[0:2].T, mod_l, gn_final, tc=tc, n_tiles=n_rows // tc,
                    n_lat_tiles=n_lat_tiles * (tm // tc), tiles_per_batch=s_len // tc, nb=nb, final=final)


def kernel(x, c, ctx, c_ctx, w_mod, b_mod, norm_mix, norm_ffn, norm_final, w_in, attn_sink, ret_decay_fwd,
           ret_decay_bwd, diff_lambda, diff_norm, w_branch, w_out, ffn_w1, ffn_w3, ffn_w2, moe_router, moe_w1,
           moe_w3, moe_w2):
    nb, s_len, d = x.shape
    c_len = ctx.shape[1]
    depth = w_in.shape[0]
    assert d == D_MODEL and nb < MOD_ROWS and s_len % 256 == 0 and s_len % c_len == 0 and c_len % 128 == 0 and s_len % RET_CHUNK == 0
    n_lat, n_ctx = nb * s_len, nb * c_len
    tm = ROW_TILE
    assert s_len % tm == 0 and n_ctx % tm == 0
    n_lat_tiles, n_all_tiles, tiles_per_batch = n_lat // tm, (n_lat + n_ctx) // tm, s_len // tm
    tile_kw = dict(tm=tm, n_lat_tiles=n_lat_tiles, tiles_per_batch=tiles_per_batch, nb=nb)

    cc = jnp.zeros((MOD_ROWS, d), F32).at[:nb].set(c).at[nb].set(c_ctx)
    mod = _modulation(cc, w_mod, b_mod)
    ptm = PROJ_ROW_TILE if s_len % PROJ_ROW_TILE == 0 and n_ctx % PROJ_ROW_TILE == 0 else tm
    proj_kw = dict(tm=ptm, n_tiles=(n_lat + n_ctx) // ptm, n_lat_tiles=n_lat // ptm, tiles_per_batch=s_len // ptm,
                   nb=nb)
    tabs = _rope_tables(s_len, ptm)
    xs = jnp.concatenate([x.reshape(n_lat, d), ctx.reshape(n_ctx, d)], axis=0)
    one_block = jnp.zeros((n_all_tiles,), I32)
    row = lambda v: v.reshape(1, -1).astype(F32)
    ffn_w = [w.astype(BF16) for w in (ffn_w1, ffn_w3, ffn_w2)]
    moe_w = [w.astype(BF16).reshape((-1,) + w.shape[2:]) for w in (moe_w1, moe_w3, moe_w2)]

    for l in range(depth):
        ctx_out = l < depth - 1
        lam_init = 0.8 - 0.6 * math.exp(-0.3 * l)
        n_tiles = n_all_tiles if ctx_out else n_lat_tiles
        proj = _project(xs, row(norm_mix[l]), mod[l], _proj_weight(w_in[l]), tabs, **proj_kw)

        sink_tab = jnp.broadcast_to(attn_sink[l].astype(F32)[:, None], (N_HEADS_A, LANES))
        dec_tab = jnp.broadcast_to(jnp.stack([ret_decay_fwd[l], ret_decay_bwd[l]]).astype(F32)[:, :, None],
                                   (2, N_HEADS_B, LANES))
        lam_p, gain = diff_lambda[l].astype(F32), row(diff_norm[l])
        kw = dict(nb=nb, s_len=s_len, c_len=c_len)
        zero_state = jnp.zeros((nb, N_HEADS_B, 2, LANES, LANES), F32)
        oa, ob, oc = (jnp.zeros((n_lat + n_ctx, BRANCH_W), BF16) for _ in range(N_BRANCH))
        ob, state = _retention(proj, dec_tab, zero_state, ob, latent=False, **kw)
        ob, _ = _retention(proj, dec_tab, state, ob, latent=True, **kw)
        if ctx_out:
            oa = _gqa(proj, sink_tab, oa, local=False, **kw)
            oc = _diff_attention(proj, lam_p, gain, oc, lam_init=lam_init, latent=False, **kw)
        oa = _gqa(proj, sink_tab, oa, local=True, **kw)
        oc = _diff_attention(proj, lam_p, gain, oc, lam_init=lam_init, latent=True, **kw)

        is_moe = l % 2 == 1
        xs, tok = _merge(oa, ob, oc, proj, w_branch[l].astype(BF16), w_out[l].astype(BF16), xs, mod[l],
                         row(norm_ffn[l]), n_tiles=n_tiles, tok_dtype=F32 if is_moe else BF16, **tile_kw)
        final = l == depth - 1
        if is_moe:
            e = l // 2
            xs = _moe(tok, xs, mod[l], moe_router[e], *moe_w, row(norm_final), n_rows=n_tiles * tm, tm=tm,
                      n_lat_tiles=n_lat_tiles, nb=nb, s_len=s_len, final=final, first_expert=e * N_EXPERTS)
        else:
            e = l // 2
            xs = _ffn(tok, *ffn_w, one_block + e, jnp.full((1,), n_tiles, I32), n_blocks=n_tiles, xs=xs,
                      mod_l=mod[l], **tile_kw)
            assert not final
    return xs[:n_lat].reshape(nb, s_len, d)
```

```python
import functools
import math

import jax
import jax.numpy as jnp
from jax import lax
from jax.experimental import pallas as pl
from jax.experimental.pallas import tpu as pltpu

F32 = jnp.float32
BF16 = jnp.bfloat16
I32 = jnp.int32

D_MODEL = 1024
GRID_W = 64
HEAD_DIM = 64
N_HEADS_A = 8
N_KV_A = 2
REP_A = N_HEADS_A // N_KV_A
WINDOW = 128
ATTN_BLOCK = 128
N_HEADS_B = 4
RET_CHUNK = 256
N_HEADS_C = 4
N_BRANCH = 3
BRANCH_W = 512
D_FF = 2816
N_EXPERTS = 8
ROPE_BASE = 10000.0
ROPE_FREQS = HEAD_DIM // 4
NORM_EPS = 1e-6
NEG_INF = -1e30
LOG2E = math.log2(math.e)

LANES = 128
MOD_ROWS = 16
G_QA, G_KA, G_QB, G_KB, G_QD, G_KD = 0, 4, 6, 8, 10, 14
N_ROPE_GROUPS = 18
G_VA, G_VB, G_GB, G_VD, G_GL = 18, 20, 24, 28, 32
PROJ_W = 56 * LANES
M_SH1, M_SC1, M_G1, M_SH2, M_SC2, M_G2 = range(6)

FF_TILE = 1408
ROW_TILE = 512
PROJ_ROW_TILE = 1024
PROJ_COL_TILE = 1792
GQA_BLOCKS_PER_STEP = 8
DIFF_Q_TILE = 1024
DIFF_K_TILE = 256
VMEM_LIMIT = 56 * 1024 * 1024


def _cparams(sem):
    return pltpu.CompilerParams(dimension_semantics=sem, vmem_limit_bytes=VMEM_LIMIT)


def _sigmoid(x):
    return 1.0 / (1.0 + jnp.exp(-x))


def _nt_dot(a, b):
    return lax.dot_general(a, b, (((1,), (1,)), ((), ())), preferred_element_type=F32)


def _head_slot(lane):
    return (lane // 32) % 2


def _rms_rows(x):
    return x * lax.rsqrt(jnp.mean(x * x, axis=-1, keepdims=True) + NORM_EPS)


def _mod_kernel(c_ref, w_ref, b_ref, o_ref):
    c = c_ref[...]
    s = c * _sigmoid(c)
    o_ref[0] = jnp.dot(s, w_ref[0], preferred_element_type=F32,
                       precision=lax.Precision.HIGHEST) + b_ref[0]


def _modulation(cc, w_mod, b_mod):
    depth, d, n = w_mod.shape
    tn = 1536
    return pl.pallas_call(
        _mod_kernel,
        grid=(depth, n // tn),
        in_specs=[pl.BlockSpec((MOD_ROWS, d), lambda l, j: (0, 0)),
                  pl.BlockSpec((1, d, tn), lambda l, j: (l, 0, j)),
                  pl.BlockSpec((1, 1, tn), lambda l, j: (l, 0, j))],
        out_specs=pl.BlockSpec((1, MOD_ROWS, tn), lambda l, j: (l, 0, j)),
        out_shape=jax.ShapeDtypeStruct((depth, MOD_ROWS, n), F32),
        compiler_params=_cparams(("arbitrary", "arbitrary")),
        name="modulation",
    )(cc, w_mod, b_mod.reshape(depth, 1, n))


def _mod_spec(chunk, nargs):
    if nargs == 1:
        return pl.BlockSpec((MOD_ROWS, D_MODEL), lambda i: (0, chunk))
    if nargs == 2:
        return pl.BlockSpec((MOD_ROWS, D_MODEL), lambda i, j: (0, chunk))
    return pl.BlockSpec((MOD_ROWS, D_MODEL), lambda i, j, *_: (0, chunk))


def _segment(i, n_lat_tiles, tiles_per_batch, nb):
    return jnp.where(i < n_lat_tiles, i // tiles_per_batch, nb)


def _proj_kernel(x_ref, gn_ref, sh_ref, sc_ref, w_ref, cos_ref, sin_ref, o_ref, h_ref, *,
                 n_lat_tiles, tiles_per_batch, nb, groups):
    i = pl.program_id(0)
    j = pl.program_id(1)

    @pl.when(j == 0)
    def _():
        seg = _segment(i, n_lat_tiles, tiles_per_batch, nb)
        sc = sc_ref[pl.ds(seg, 1), :]
        sh = sh_ref[pl.ds(seg, 1), :]
        h = (_rms_rows(x_ref[...]) * gn_ref[...]) * (1.0 + sc) + sh
        h_ref[...] = h.astype(BF16)

    acc = jnp.dot(h_ref[...], w_ref[...], preferred_element_type=F32)
    for g in range(groups):
        cols = slice(g * LANES, (g + 1) * LANES)
        a = acc[:, cols]
        y = a * cos_ref[...] + pltpu.roll(a, LANES // 2, 1) * sin_ref[...]
        o_ref[:, cols] = jnp.where(j * groups + g < N_ROPE_GROUPS, y, a).astype(BF16)


def _project(xs, gn, mod_l, w, tabs, *, tm, n_tiles, n_lat_tiles, tiles_per_batch, nb):
    tn = PROJ_COL_TILE
    groups = tn // LANES
    rope_map = lambda i, j: (jnp.where(i < n_lat_tiles, i % tiles_per_batch, tiles_per_batch), 0)
    kern = functools.partial(_proj_kernel, n_lat_tiles=n_lat_tiles, tiles_per_batch=tiles_per_batch,
                             nb=nb, groups=groups)
    return pl.pallas_call(
        kern,
        grid=(n_tiles, PROJ_W // tn),
        in_specs=[pl.BlockSpec((tm, D_MODEL), lambda i, j: (i, 0)),
                  pl.BlockSpec((1, D_MODEL), lambda i, j: (0, 0)),
                  _mod_spec(M_SH1, 2), _mod_spec(M_SC1, 2),
                  pl.BlockSpec((D_MODEL, tn), lambda i, j: (0, j)),
                  pl.BlockSpec((tm, LANES), rope_map),
                  pl.BlockSpec((tm, LANES), rope_map)],
        out_specs=pl.BlockSpec((tm, tn), lambda i, j: (i, j)),
        out_shape=jax.ShapeDtypeStruct((xs.shape[0], PROJ_W), BF16),
        scratch_shapes=[pltpu.VMEM((tm, D_MODEL), BF16)],
        compiler_params=_cparams(("arbitrary", "arbitrary")),
        name="in_proj",
    )(xs, gn, mod_l, mod_l, w, *tabs)


def _rope_tables(s, tm):
    rows = s // GRID_W
    row = jnp.repeat(jnp.arange(rows, dtype=F32), GRID_W)
    col = jnp.tile(jnp.arange(GRID_W, dtype=F32), rows)
    inv = 1.0 / (ROPE_BASE ** (jnp.arange(ROPE_FREQS, dtype=F32) / ROPE_FREQS))
    ar, ac = row[:, None] * inv, col[:, None] * inv
    cos = jnp.tile(jnp.concatenate([jnp.cos(ar), jnp.cos(ac)], axis=1), (1, 4))
    sin = jnp.tile(jnp.concatenate([jnp.sin(ar), jnp.sin(ac)], axis=1), (1, 2))
    sin = jnp.concatenate([-sin, sin], axis=1)
    ident = lambda t, v: jnp.concatenate([t, jnp.full((tm, LANES), v, F32)], axis=0)
    return ident(cos, 1.0), ident(sin, 0.0)


def _proj_weight(w):
    sl = lambda a, b: w[:, a:b]
    sc = HEAD_DIM ** -0.5
    dup = lambda a: jnp.concatenate([sl(a, a + 64), sl(a, a + 64), sl(a + 64, a + 128), sl(a + 64, a + 128)], axis=1)
    parts = [sl(0, 512) * (sc * LOG2E), dup(512), sl(768, 1024), sl(1024, 1280) * sc, sl(2304, 2816) * (sc * LOG2E),
             sl(2816, 3328), dup(640), sl(1280, 1792), sl(1792, 2304), sl(3328, 3840), sl(3840, 6912)]
    n_rope = N_ROPE_GROUPS * LANES
    rope = jnp.concatenate(parts[:6], axis=1).reshape(w.shape[0], N_ROPE_GROUPS, 2, 2, 2, ROPE_FREQS)
    rope = rope.transpose(0, 1, 4, 2, 3, 5).reshape(w.shape[0], n_rope)
    return jnp.concatenate([rope] + parts[6:], axis=1).astype(BF16)


def _gqa_kernel(*refs, local, s_len, tq, n_qb):
    if local:
        q_ref = refs[0]
        k_refs, kx_ref = refs[1:n_qb + 3], refs[n_qb + 3]
        v_refs, vx_ref = refs[n_qb + 4:2 * n_qb + 6], refs[2 * n_qb + 6]
        sink_ref, _, o_ref = refs[2 * n_qb + 7:]
    else:
        q_ref, kx_ref, vx_ref, sink_ref, _, o_ref = refs
    lane = lax.broadcasted_iota(I32, (tq, LANES), 1)
    lo = lane < HEAD_DIM
    slot0 = _head_slot(lane) == 0
    if local:
        nk = 3 * tq + kx_ref.shape[0]
        qi = lax.broadcasted_iota(I32, (tq, nk), 0)
        kj = lax.broadcasted_iota(I32, (tq, nk), 1)
        rel = kj - qi
        off_band = (rel < 0) | (rel > 2 * WINDOW)
    for qb in range(n_qb):
        rows = slice(qb * tq, (qb + 1) * tq)
        if local:
            kpos = (pl.program_id(1) * n_qb + qb) * tq - WINDOW + kj
            bad = (off_band | (kpos < 0) | (kpos >= s_len)) & (kj < 3 * tq)
        for g in range(N_KV_A):
            cols = slice(g * LANES, (g + 1) * LANES)
            if local:
                kcat = jnp.concatenate([r[:, cols] for r in k_refs[qb:qb + 3]] + [kx_ref[:, cols]], axis=0)
                vcat = jnp.concatenate([r[:, cols] for r in v_refs[qb:qb + 3]] + [vx_ref[:, cols]], axis=0)
            else:
                kcat, vcat = kx_ref[:, cols], vx_ref[:, cols]
            qs = []
            for r in range(REP_A):
                h = g * REP_A + r
                qp = q_ref[rows, (h // 2) * LANES:(h // 2 + 1) * LANES]
                qs.append(jnp.where(slot0 if h % 2 == 0 else jnp.logical_not(slot0), qp, jnp.zeros_like(qp)))
            s = _nt_dot(jnp.concatenate(qs, axis=0), kcat)
            ps, dens = [], []
            for r in range(REP_A):
                h = g * REP_A + r
                sr = s[r * tq:(r + 1) * tq]
                if local:
                    sr = jnp.where(bad, NEG_INF, sr)
                sk = sink_ref[h:h + 1, 0:1] * LOG2E
                m = jnp.maximum(jnp.max(sr, axis=-1, keepdims=True), sk)
                p = jnp.exp2(sr - m)
                dens.append(jnp.sum(p, axis=-1, keepdims=True) + jnp.exp2(sk - m))
                ps.append(p.astype(BF16))
            o = jnp.dot(jnp.concatenate(ps, axis=0), vcat, preferred_element_type=F32)
            outs = [o[r * tq:(r + 1) * tq] / dens[r] for r in range(REP_A)]
            for pr in range(REP_A // 2):
                pair = jnp.where(lo, outs[2 * pr], outs[2 * pr + 1])
                c0 = (g * (REP_A // 2) + pr) * LANES
                o_ref[rows, c0:c0 + LANES] = pair.astype(BF16)


def _gqa(proj, sink_tab, o_prev, *, nb, s_len, c_len, local):
    n_rows = proj.shape[0]
    cblk = nb * s_len // c_len
    kcol, vcol = G_KA // 2, G_VA // 2
    out_shape = jax.ShapeDtypeStruct((n_rows, N_HEADS_A * HEAD_DIM), BF16)
    if local:
        tq = ATTN_BLOCK
        nq = s_len // tq
        n_qb = GQA_BLOCKS_PER_STEP if nq % GQA_BLOCKS_PER_STEP == 0 else 1
        specs = [pl.BlockSpec((n_qb * tq, 4 * LANES), lambda b, i: (b * (nq // n_qb) + i, 0))]
        for col in (kcol, vcol):
            specs += [pl.BlockSpec((tq, 2 * LANES),
                                   lambda b, i, col=col, j=j: (b * nq + jnp.clip(n_qb * i - 1 + j, 0, nq - 1), col))
                      for j in range(n_qb + 2)]
            specs.append(pl.BlockSpec((c_len, 2 * LANES), lambda b, i, col=col: (cblk + b, col)))
        specs += [pl.BlockSpec((N_HEADS_A, LANES), lambda b, i: (0, 0)),
                  pl.BlockSpec(memory_space=pl.ANY)]
        args = [proj] * (2 * n_qb + 7) + [sink_tab, o_prev]
        return pl.pallas_call(
            functools.partial(_gqa_kernel, local=True, s_len=s_len, tq=tq, n_qb=n_qb),
            grid=(nb, nq // n_qb), in_specs=specs,
            out_specs=pl.BlockSpec((n_qb * tq, 4 * LANES), lambda b, i: (b * (nq // n_qb) + i, 0)),
            out_shape=out_shape, input_output_aliases={2 * n_qb + 8: 0},
            compiler_params=_cparams(("arbitrary", "arbitrary")), name="gqa_window",
        )(*args)
    specs = [pl.BlockSpec((c_len, 4 * LANES), lambda b: (cblk + b, 0)),
             pl.BlockSpec((c_len, 2 * LANES), lambda b: (cblk + b, kcol)),
             pl.BlockSpec((c_len, 2 * LANES), lambda b: (cblk + b, vcol)),
             pl.BlockSpec((N_HEADS_A, LANES), lambda b: (0, 0)),
             pl.BlockSpec(memory_space=pl.ANY)]
    return pl.pallas_call(
        functools.partial(_gqa_kernel, local=False, s_len=s_len, tq=c_len, n_qb=1),
        grid=(nb,), in_specs=specs,
        out_specs=pl.BlockSpec((c_len, 4 * LANES), lambda b: (cblk + b, 0)),
        out_shape=out_shape, input_output_aliases={4: 0},
        compiler_params=_cparams(("arbitrary",)), name="gqa_context",
    )(proj, proj, proj, sink_tab, o_prev)


def _ret_kernel(q_ref, k_ref, v_ref, g_ref, dec_ref, sin_ref, _, o_ref, sout_ref, of_ref, ob_ref, *, n_rows):
    h = pl.program_id(1)
    c = min(RET_CHUNK, n_rows)
    nch = n_rows // c
    lane = lax.broadcasted_iota(I32, (c, LANES), 1)
    row = lax.broadcasted_iota(I32, (c, LANES), 0)
    qmask = _head_slot(lane) == (h % 2)

    def log_sigmoid(d):
        x = dec_ref[d, pl.ds(h, 1), :]
        return jnp.minimum(x, 0.0) - jnp.log(1.0 + jnp.exp(-jnp.abs(x)))

    lg_f, lg_b = log_sigmoid(0), log_sigmoid(1)
    diff = lax.broadcasted_iota(I32, (c, c), 0) - lax.broadcasted_iota(I32, (c, c), 1)
    rowf = row.astype(F32)
    dmat_f = jnp.where(diff >= 0, jnp.exp(jnp.maximum(diff, 0).astype(F32) * lg_f[:, 0:1]), 0.0)
    dmat_b = jnp.where(diff <= 0, jnp.exp(jnp.maximum(-diff, 0).astype(F32) * lg_b[:, 0:1]), 0.0)
    qdec_f = jnp.exp((rowf + 1.0) * lg_f)
    kdec_f = jnp.exp((c - 1.0 - rowf) * lg_f)
    qdec_b = jnp.exp((c - rowf) * lg_b)
    kdec_b = jnp.exp(rowf * lg_b)
    cdec_f = jnp.exp(c * lg_f)
    cdec_b = jnp.exp(c * lg_b)

    def chunk(n, state, dmat, qdec, kdec, cdec, dst_ref):
        rows = pl.ds(pl.multiple_of(n * c, c), c)
        q = q_ref[rows, :]
        q = jnp.where(qmask, q, jnp.zeros_like(q))
        k = k_ref[rows, :]
        v = v_ref[rows, :]
        sc = (_nt_dot(q, k) * dmat).astype(BF16)
        intra = jnp.dot(sc, v, preferred_element_type=F32)
        inter = jnp.dot((q.astype(F32) * qdec).astype(BF16), state.astype(BF16), preferred_element_type=F32)
        dst_ref[rows, :] = intra + inter
        kd = (k.astype(F32) * kdec).T.astype(BF16)
        return state * cdec + jnp.dot(kd, v, preferred_element_type=F32)

    def body(n, carry):
        sf, sb = carry
        sf = chunk(n, sf, dmat_f, qdec_f, kdec_f, cdec_f, of_ref)
        sb = chunk(nch - 1 - n, sb, dmat_b, qdec_b, kdec_b, cdec_b, ob_ref)
        return sf, sb

    sf, sb = lax.fori_loop(0, nch, body, (sin_ref[0, 0, 0], sin_ref[0, 0, 1]), unroll=4)
    sout_ref[0, 0, 0] = sf
    sout_ref[0, 0, 1] = sb
    o = of_ref[...] + ob_ref[...]
    gate = g_ref[...].astype(F32)
    o_ref[...] = (_rms_rows(o) * (gate * _sigmoid(gate))).astype(BF16)


def _retention(proj, dec_tab, s_in, o_prev, *, nb, s_len, c_len, latent):
    n_rows = s_len if latent else c_len
    base = 0 if latent else nb * s_len // c_len
    rb = lambda b: base + b
    col = lambda g0, per_pair: (lambda b, h: (rb(b), g0 + (h // 2 if per_pair else h)))
    state_spec = pl.BlockSpec((1, 1, 2, LANES, LANES), lambda b, h: (b, h, 0, 0, 0))
    specs = [pl.BlockSpec((n_rows, LANES), col(G_QB, True)),
             pl.BlockSpec((n_rows, LANES), col(G_KB, True)),
             pl.BlockSpec((n_rows, LANES), col(G_VB, False)),
             pl.BlockSpec((n_rows, LANES), col(G_GB, False)),
             pl.BlockSpec((2, N_HEADS_B, LANES), lambda b, h: (0, 0, 0)),
             state_spec, pl.BlockSpec(memory_space=pl.ANY)]
    args = [proj, proj, proj, proj, dec_tab, s_in, o_prev]
    aliases = {6: 0}
    return pl.pallas_call(
        functools.partial(_ret_kernel, n_rows=n_rows),
        grid=(nb, N_HEADS_B), in_specs=specs,
        out_specs=[pl.BlockSpec((n_rows, LANES), lambda b, h: (rb(b), h)), state_spec],
        out_shape=[jax.ShapeDtypeStruct((proj.shape[0], N_HEADS_B * LANES), BF16),
                   jax.ShapeDtypeStruct(s_in.shape, F32)],
        scratch_shapes=[pltpu.VMEM((n_rows, LANES), F32), pltpu.VMEM((n_rows, LANES), F32)],
        input_output_aliases=aliases,
        compiler_params=_cparams(("arbitrary", "arbitrary")),
        name="retention_latent" if latent else "retention_context",
    )(*args)


def _diff_kernel(*refs, latent, lam_init):
    if latent:
        q_ref, kl_ref, kx_ref, vl_ref, vx_ref, lam_ref, gain_ref, _, o_ref, k_scr, v_scr = refs
    else:
        q_ref, kx_ref, vx_ref, lam_ref, gain_ref, _, o_ref, k_scr, v_scr = refs
    nl = kl_ref.shape[0] if latent else 0

    def fill():
        if latent:
            k_scr[0:nl, :] = kl_ref[...]
            v_scr[0:nl, 0:LANES] = vl_ref[...]
        k_scr[nl:, :] = kx_ref[...]
        v_scr[nl:, 0:LANES] = vx_ref[...]
        v_scr[:, LANES:] = jnp.ones((v_scr.shape[0], LANES), BF16)

    if latent:
        pl.when(pl.program_id(2) == 0)(fill)
    else:
        fill()
    lp = lam_ref[...]
    lam = (jnp.exp(jnp.sum(lp[0:1] * lp[1:2], axis=-1, keepdims=True))
           - jnp.exp(jnp.sum(lp[2:3] * lp[3:4], axis=-1, keepdims=True)) + lam_init)
    q = q_ref[...]
    lo = _head_slot(lax.broadcasted_iota(I32, q.shape, 1)) == 0
    zero = jnp.zeros_like(q)
    n_keys = k_scr.shape[0]
    outs = []
    for comp in range(2):
        qc = jnp.where(lo, q, zero) if comp == 0 else jnp.where(lo, zero, q)
        m = acc = None
        for start in range(0, n_keys, DIFF_K_TILE):
            size = min(DIFF_K_TILE, n_keys - start)
            s = _nt_dot(qc, k_scr[start:start + size, :])
            m_t = jnp.max(s, axis=-1, keepdims=True)
            m_new = m_t if m is None else jnp.maximum(m, m_t)
            p = jnp.exp2(s - m_new).astype(BF16)
            pv = jnp.dot(p, v_scr[start:start + size, :], preferred_element_type=F32)
            acc = pv if acc is None else jnp.exp2(m - m_new) * acc + pv
            m = m_new
        outs.append(acc[:, :LANES] / acc[:, LANES:])
    o = outs[0] - lam * outs[1]
    o_ref[...] = (_rms_rows(o) * gain_ref[...] * (1.0 - lam_init)).astype(BF16)


def _diff_attention(proj, lam_p, gain, o_prev, *, nb, s_len, c_len, lam_init, latent):
    cblk = nb * s_len // c_len
    out_shape = jax.ShapeDtypeStruct((proj.shape[0], N_HEADS_C * LANES), BF16)
    small = [pl.BlockSpec((4, HEAD_DIM), lambda *_: (0, 0)), pl.BlockSpec((1, LANES), lambda *_: (0, 0))]
    if latent:
        tq = min(DIFF_Q_TILE, s_len)
        nq = s_len // tq
        specs = [pl.BlockSpec((tq, LANES), lambda b, h, i: (b * nq + i, G_QD + h)),
                 pl.BlockSpec((s_len, LANES), lambda b, h, i: (b, G_KD + h)),
                 pl.BlockSpec((c_len, LANES), lambda b, h, i: (cblk + b, G_KD + h)),
                 pl.BlockSpec((s_len, LANES), lambda b, h, i: (b, G_VD + h)),
                 pl.BlockSpec((c_len, LANES), lambda b, h, i: (cblk + b, G_VD + h))] + small
        specs.append(pl.BlockSpec(memory_space=pl.ANY))
        return pl.pallas_call(
            functools.partial(_diff_kernel, latent=True, lam_init=lam_init),
            grid=(nb, N_HEADS_C, nq), in_specs=specs,
            out_specs=pl.BlockSpec((tq, LANES), lambda b, h, i: (b * nq + i, h)),
            out_shape=out_shape, input_output_aliases={7: 0},
            scratch_shapes=[pltpu.VMEM((s_len + c_len, LANES), BF16), pltpu.VMEM((s_len + c_len, 2 * LANES), BF16)],
            compiler_params=_cparams(("arbitrary", "arbitrary", "arbitrary")), name="diff_latent",
        )(proj, proj, proj, proj, proj, lam_p, gain, o_prev)
    specs = [pl.BlockSpec((c_len, LANES), lambda b, h: (cblk + b, G_QD + h)),
             pl.BlockSpec((c_len, LANES), lambda b, h: (cblk + b, G_KD + h)),
             pl.BlockSpec((c_len, LANES), lambda b, h: (cblk + b, G_VD + h))] + small
    specs.append(pl.BlockSpec(memory_space=pl.ANY))
    return pl.pallas_call(
        functools.partial(_diff_kernel, latent=False, lam_init=lam_init),
        grid=(nb, N_HEADS_C), in_specs=specs,
        out_specs=pl.BlockSpec((c_len, LANES), lambda b, h: (cblk + b, h)),
        out_shape=out_shape, input_output_aliases={5: 0},
        scratch_shapes=[pltpu.VMEM((c_len, LANES), BF16), pltpu.VMEM((c_len, 2 * LANES), BF16)],
        compiler_params=_cparams(("arbitrary", "arbitrary")), name="diff_context",
    )(proj, proj, proj, lam_p, gain, o_prev)


def _merge_kernel(oa_ref, ob_ref, oc_ref, ga_ref, gb_ref, gc_ref, wb_ref, wo_ref, x_ref, g1_ref, sh_ref, sc_ref,
                  gn_ref, xo_ref, tok_ref, *, n_lat_tiles, tiles_per_batch, nb):
    i = pl.program_id(0)
    seg = _segment(i, n_lat_tiles, tiles_per_batch, nb)
    y = None
    for k, (o_ref, gl_ref) in enumerate(((oa_ref, ga_ref), (ob_ref, gb_ref), (oc_ref, gc_ref))):
        t = _sigmoid(gl_ref[...].astype(F32)) * jnp.dot(o_ref[...], wb_ref[k], preferred_element_type=F32)
        y = t if y is None else y + t
    z = jnp.dot(y.astype(BF16), wo_ref[...], preferred_element_type=F32)
    xn = x_ref[...] + g1_ref[pl.ds(seg, 1), :] * z
    xo_ref[...] = xn
    tok = (_rms_rows(xn) * gn_ref[...]) * (1.0 + sc_ref[pl.ds(seg, 1), :]) + sh_ref[pl.ds(seg, 1), :]
    tok_ref[...] = tok.astype(tok_ref.dtype)


def _merge(oa, ob, oc, proj, wb, wo, xs, mod_l, gn, *, tm, n_tiles, n_lat_tiles, tiles_per_batch, nb, tok_dtype):
    row = lambda w: pl.BlockSpec((tm, w), lambda i: (i, 0))
    gl = lambda k: pl.BlockSpec((tm, D_MODEL), lambda i: (i, G_GL // 8 + k))
    return pl.pallas_call(
        functools.partial(_merge_kernel, n_lat_tiles=n_lat_tiles, tiles_per_batch=tiles_per_batch, nb=nb),
        grid=(n_tiles,),
        in_specs=[row(BRANCH_W), row(BRANCH_W), row(BRANCH_W), gl(0), gl(1), gl(2),
                  pl.BlockSpec((N_BRANCH, BRANCH_W, D_MODEL), lambda i: (0, 0, 0)),
                  pl.BlockSpec((D_MODEL, D_MODEL), lambda i: (0, 0)),
                  row(D_MODEL), _mod_spec(M_G1, 1), _mod_spec(M_SH2, 1), _mod_spec(M_SC2, 1),
                  pl.BlockSpec((1, D_MODEL), lambda i: (0, 0))],
        out_specs=[row(D_MODEL), row(D_MODEL)],
        out_shape=[jax.ShapeDtypeStruct(xs.shape, F32), jax.ShapeDtypeStruct((n_tiles * tm, D_MODEL), tok_dtype)],
        input_output_aliases={8: 0},
        compiler_params=_cparams(("arbitrary",)), name="merge",
    )(oa, ob, oc, proj, proj, proj, wb, wo, xs, mod_l, mod_l, mod_l, gn)


def _ffn_kernel(be_ref, nu_ref, *refs, dense, n_lat_tiles, tiles_per_batch, nb):
    if dense:
        t_ref, w1_ref, w3_ref, w2_ref, x_ref, g2_ref, o_ref, acc_ref = refs
    else:
        t_ref, w1_ref, w3_ref, w2_ref, o_ref, acc_ref = refs
    i = pl.program_id(0)
    f = pl.program_id(1)
    last = pl.num_programs(1) - 1
    used = i < nu_ref[0]

    @pl.when(used)
    def _():
        t = t_ref[...].astype(BF16)
        h1 = jnp.dot(t, w1_ref[0], preferred_element_type=F32)
        h3 = jnp.dot(t, w3_ref[0], preferred_element_type=F32)
        a = ((h1 * _sigmoid(h1)) * h3).astype(BF16)
        part = jnp.dot(a, w2_ref[0], preferred_element_type=F32)

        @pl.when(f == 0)
        def _():
            acc_ref[...] = part

        @pl.when(f > 0)
        def _():
            acc_ref[...] += part

    @pl.when(f == last)
    def _():
        if dense:
            seg = _segment(i, n_lat_tiles, tiles_per_batch, nb)
            o_ref[...] = x_ref[...] + g2_ref[pl.ds(seg, 1), :] * acc_ref[...]
        else:
            o_ref[...] = jnp.where(used, acc_ref[...], 0.0)


def _ffn(tok, w1, w3, w2, block_expert, n_used, *, tm, n_blocks, xs=None, mod_l=None,
         n_lat_tiles=0, tiles_per_batch=1, nb=0):
    dense = xs is not None
    nf = D_FF // FF_TILE
    specs = [pl.BlockSpec((tm, D_MODEL), lambda i, f, be, nu: (i, 0)),
             pl.BlockSpec((1, D_MODEL, FF_TILE), lambda i, f, be, nu: (be[i], 0, f)),
             pl.BlockSpec((1, D_MODEL, FF_TILE), lambda i, f, be, nu: (be[i], 0, f)),
             pl.BlockSpec((1, FF_TILE, D_MODEL), lambda i, f, be, nu: (be[i], f, 0))]
    args = [tok, w1, w3, w2]
    aliases = {}
    if dense:
        specs += [pl.BlockSpec((tm, D_MODEL), lambda i, f, be, nu: (i, 0)), _mod_spec(M_G2, 4)]
        args += [xs, mod_l]
        aliases = {6: 0}
        out_rows = xs.shape[0]
    else:
        out_rows = tok.shape[0]
    return pl.pallas_call(
        functools.partial(_ffn_kernel, dense=dense, n_lat_tiles=n_lat_tiles, tiles_per_batch=tiles_per_batch, nb=nb),
        grid_spec=pltpu.PrefetchScalarGridSpec(
            num_scalar_prefetch=2, grid=(n_blocks, nf), in_specs=specs,
            out_specs=pl.BlockSpec((tm, D_MODEL), lambda i, f, be, nu: (i, 0)),
            scratch_shapes=[pltpu.VMEM((tm, D_MODEL), F32)]),
        out_shape=jax.ShapeDtypeStruct((out_rows, D_MODEL), F32),
        input_output_aliases=aliases,
        compiler_params=_cparams(("arbitrary", "arbitrary")),
        name="ffn_dense" if dense else "ffn_experts",
    )(block_expert, n_used, *args)


def _router_kernel(tok_ref, wr_ref, ids_ref, gate_ref, cnt_ref, base_ref, *, tr):
    i = pl.program_id(0)

    @pl.when(i == 0)
    def _():
        base_ref[...] = jnp.zeros_like(base_ref)

    logits = _nt_dot(wr_ref[...], tok_ref[...].astype(BF16))
    eidx = lax.broadcasted_iota(I32, logits.shape, 0)
    m1 = jnp.max(logits, axis=0, keepdims=True)
    i1 = jnp.min(jnp.where(logits == m1, eidx, N_EXPERTS), axis=0, keepdims=True)
    rest = jnp.where(eidx == i1, -jnp.inf, logits)
    m2 = jnp.max(rest, axis=0, keepdims=True)
    i2 = jnp.min(jnp.where(rest == m2, eidx, N_EXPERTS), axis=0, keepdims=True)
    e = jnp.exp(m2 - m1)
    sel1 = eidx == i1
    sel2 = eidx == i2
    both = jnp.where(sel1 | sel2, 1.0, 0.0)
    before = lax.broadcasted_iota(I32, (tr, tr), 0) < lax.broadcasted_iota(I32, (tr, tr), 1)
    prefix = jnp.dot(both.astype(BF16), jnp.where(before, 1.0, 0.0).astype(BF16), preferred_element_type=F32)
    pos = base_ref[:, 0:1] + prefix
    r1 = jnp.sum(jnp.where(sel1, pos, 0.0), axis=0, keepdims=True)
    r2 = jnp.sum(jnp.where(sel2, pos, 0.0), axis=0, keepdims=True)
    ids_ref[0:1, :] = i1
    ids_ref[1:2, :] = i2
    ids_ref[2:3, :] = r1.astype(I32)
    ids_ref[3:4, :] = r2.astype(I32)
    ids_ref[4:8, :] = jnp.zeros((4, tr), I32)
    gate_ref[0:1, :] = 1.0 / (1.0 + e)
    gate_ref[1:2, :] = e / (1.0 + e)
    gate_ref[2:8, :] = jnp.zeros((6, tr), F32)
    base_ref[...] = base_ref[...] + jnp.sum(both, axis=1, keepdims=True)
    cnt_ref[...] = base_ref[...].astype(I32)


def _router(tok, wr, *, n_rows, tr):
    return pl.pallas_call(
        functools.partial(_router_kernel, tr=tr),
        grid=(n_rows // tr,),
        in_specs=[pl.BlockSpec((tr, D_MODEL), lambda i: (i, 0)),
                  pl.BlockSpec((N_EXPERTS, D_MODEL), lambda i: (0, 0))],
        out_specs=[pl.BlockSpec((8, tr), lambda i: (0, i)), pl.BlockSpec((8, tr), lambda i: (0, i)),
                   pl.BlockSpec((N_EXPERTS, LANES), lambda i: (0, 0))],
        out_shape=[jax.ShapeDtypeStruct((8, n_rows), I32), jax.ShapeDtypeStruct((8, n_rows), F32),
                   jax.ShapeDtypeStruct((N_EXPERTS, LANES), I32)],
        scratch_shapes=[pltpu.VMEM((N_EXPERTS, LANES), F32)],
        compiler_params=_cparams(("arbitrary",)), name="router",
    )(tok, wr)


def _scatter_kernel(d0_ref, d1_ref, tok_ref, _, o_hbm, sem, *, tb):
    i = pl.program_id(0)

    def issue(r, carry):
        t = i * tb + r
        src = tok_ref.at[pl.ds(r, 1), :]
        pltpu.make_async_copy(src, o_hbm.at[pl.ds(d0_ref[t], 1), :], sem).start()
        pltpu.make_async_copy(src, o_hbm.at[pl.ds(d1_ref[t], 1), :], sem).start()
        return carry

    lax.fori_loop(0, tb, issue, 0, unroll=8)
    pltpu.make_async_copy(tok_ref, o_hbm.at[pl.ds(0, tb), :], sem).wait()
    pltpu.make_async_copy(tok_ref, o_hbm.at[pl.ds(0, tb), :], sem).wait()


def _dispatch(tok, dest0, dest1, *, n_rows, n_slots):
    tb = 2 * ROW_TILE if n_rows % (2 * ROW_TILE) == 0 else ROW_TILE
    return pl.pallas_call(
        functools.partial(_scatter_kernel, tb=tb),
        grid_spec=pltpu.PrefetchScalarGridSpec(
            num_scalar_prefetch=2, grid=(n_rows // tb,),
            in_specs=[pl.BlockSpec((tb, D_MODEL), lambda i, d0, d1: (i, 0)), pl.BlockSpec(memory_space=pl.ANY)],
            out_specs=pl.BlockSpec(memory_space=pl.ANY),
            scratch_shapes=[pltpu.SemaphoreType.DMA(())]),
        out_shape=jax.ShapeDtypeStruct((n_slots, D_MODEL), tok.dtype),
        input_output_aliases={3: 0},
        compiler_params=_cparams(("arbitrary",)), name="moe_dispatch",
    )(dest0, dest1, tok, jnp.zeros((n_slots, D_MODEL), tok.dtype))


def _combine_kernel(d0_ref, d1_ref, y_hbm, x_ref, gate_ref, g2_ref, gn_ref, o_ref, buf_ref, sem, *,
                    tc, n_lat_tiles, tiles_per_batch, nb, final):
    i = pl.program_id(0)

    def issue(r, carry):
        pltpu.make_async_copy(y_hbm.at[pl.ds(d0_ref[i * tc + r], 1), :], buf_ref.at[0, pl.ds(r, 1), :], sem).start()
        pltpu.make_async_copy(y_hbm.at[pl.ds(d1_ref[i * tc + r], 1), :], buf_ref.at[1, pl.ds(r, 1), :], sem).start()
        return carry

    lax.fori_loop(0, tc, issue, 0, unroll=8)
    pltpu.make_async_copy(y_hbm.at[pl.ds(0, tc), :], buf_ref.at[0], sem).wait()
    pltpu.make_async_copy(y_hbm.at[pl.ds(0, tc), :], buf_ref.at[1], sem).wait()
    seg = _segment(i, n_lat_tiles, tiles_per_batch, nb)
    gates = gate_ref[...]
    f = gates[:, 0:1] * buf_ref[0] + gates[:, 1:2] * buf_ref[1]
    xo = x_ref[...] + g2_ref[pl.ds(seg, 1), :] * f
    if final:
        xo = _rms_rows(xo) * gn_ref[...]
    o_ref[...] = xo


def _combine(dest0, dest1, y, xs, gates, mod_l, gn, *, tc, n_tiles, n_lat_tiles, tiles_per_batch, nb, final):
    out_rows = n_tiles * tc
    return pl.pallas_call(
        functools.partial(_combine_kernel, tc=tc, n_lat_tiles=n_lat_tiles, tiles_per_batch=tiles_per_batch,
                          nb=nb, final=final),
        grid_spec=pltpu.PrefetchScalarGridSpec(
            num_scalar_prefetch=2, grid=(n_tiles,),
            in_specs=[pl.BlockSpec(memory_space=pl.ANY),
                      pl.BlockSpec((tc, D_MODEL), lambda i, a, b: (i, 0)),
                      pl.BlockSpec((tc, 2), lambda i, a, b: (i, 0)),
                      _mod_spec(M_G2, 3),
                      pl.BlockSpec((1, D_MODEL), lambda i, a, b: (0, 0))],
            out_specs=pl.BlockSpec((tc, D_MODEL), lambda i, a, b: (i, 0)),
            scratch_shapes=[pltpu.VMEM((2, tc, D_MODEL), F32), pltpu.SemaphoreType.DMA(())]),
        out_shape=jax.ShapeDtypeStruct((out_rows, D_MODEL), F32),
        compiler_params=_cparams(("arbitrary",)), name="moe_combine",
    )(dest0, dest1, y, xs, gates, mod_l, gn)


def _moe(tok, xs, mod_l, w_router, w1, w3, w2, gn_final, *, n_rows, tm, n_lat_tiles, nb, s_len, final, first_expert):
    bs = ROW_TILE
    ids, gates, counts = _router(tok, w_router.T.astype(BF16), n_rows=n_rows, tr=tm)
    counts = counts[:, 0]
    padded = (counts + bs - 1) // bs * bs
    pad_ends = jnp.cumsum(padded)
    starts = pad_ends - padded
    dest0 = starts[ids[0]] + ids[2]
    dest1 = starts[ids[1]] + ids[3]
    n_blocks = (2 * n_rows + bs - 1) // bs + N_EXPERTS
    block_start = jnp.arange(n_blocks, dtype=I32) * bs
    block_expert = jnp.minimum(jnp.sum((pad_ends[None, :] <= block_start[:, None]).astype(I32), axis=1),
                               N_EXPERTS - 1) + first_expert
    n_used = (pad_ends[-1:] // bs).astype(I32)
    buf = _dispatch(tok, dest0, dest1, n_rows=n_rows, n_slots=n_blocks * bs)
    y = _ffn(buf, w1, w3, w2, block_expert, n_used, tm=bs, n_blocks=n_blocks)
    tc = ROW_TILE
    return _combine(dest0, dest1, y, xs, gates[0:2].T, mod_l, gn_final, tc=tc, n_tiles=n_rows // tc,
                    n_lat_tiles=n_lat_tiles * (tm // tc), tiles_per_batch=s_len // tc, nb=nb, final=final)


def kernel(x, c, ctx, c_ctx, w_mod, b_mod, norm_mix, norm_ffn, norm_final, w_in, attn_sink, ret_decay_fwd,
           ret_decay_bwd, diff_lambda, diff_norm, w_branch, w_out, ffn_w1, ffn_w3, ffn_w2, moe_router, moe_w1,
           moe_w3, moe_w2):
    nb, s_len, d = x.shape
    c_len = ctx.shape[1]
    depth = w_in.shape[0]
    assert d == D_MODEL and nb < MOD_ROWS and s_len % 256 == 0 and s_len % c_len == 0 and c_len % 128 == 0 and s_len % RET_CHUNK == 0
    n_lat, n_ctx = nb * s_len, nb * c_len
    tm = ROW_TILE
    assert s_len % tm == 0 and n_ctx % tm == 0
    n_lat_tiles, n_all_tiles, tiles_per_batch = n_lat // tm, (n_lat + n_ctx) // tm, s_len // tm
    tile_kw = dict(tm=tm, n_lat_tiles=n_lat_tiles, tiles_per_batch=tiles_per_batch, nb=nb)

    cc = jnp.zeros((MOD_ROWS, d), F32).at[:nb].set(c).at[nb].set(c_ctx)
    mod = _modulation(cc, w_mod, b_mod)
    ptm = PROJ_ROW_TILE if s_len % PROJ_ROW_TILE == 0 and n_ctx % PROJ_ROW_TILE == 0 else tm
    proj_kw = dict(tm=ptm, n_tiles=(n_lat + n_ctx) // ptm, n_lat_tiles=n_lat // ptm, tiles_per_batch=s_len // ptm,
                   nb=nb)
    tabs = _rope_tables(s_len, ptm)
    xs = jnp.concatenate([x.reshape(n_lat, d), ctx.reshape(n_ctx, d)], axis=0)
    one_block = jnp.zeros((n_all_tiles,), I32)
    row = lambda v: v.reshape(1, -1).astype(F32)
    ffn_w = [w.astype(BF16) for w in (ffn_w1, ffn_w3, ffn_w2)]
    moe_w = [w.astype(BF16).reshape((-1,) + w.shape[2:]) for w in (moe_w1, moe_w3, moe_w2)]

    for l in range(depth):
        ctx_out = l < depth - 1
        lam_init = 0.8 - 0.6 * math.exp(-0.3 * l)
        n_tiles = n_all_tiles if ctx_out else n_lat_tiles
        proj = _project(xs, row(norm_mix[l]), mod[l], _proj_weight(w_in[l]), tabs, **proj_kw)

        sink_tab = jnp.broadcast_to(attn_sink[l].astype(F32)[:, None], (N_HEADS_A, LANES))
        dec_tab = jnp.broadcast_to(jnp.stack([ret_decay_fwd[l], ret_decay_bwd[l]]).astype(F32)[:, :, None],
                                   (2, N_HEADS_B, LANES))
        lam_p, gain = diff_lambda[l].astype(F32), row(diff_norm[l])
        kw = dict(nb=nb, s_len=s_len, c_len=c_len)
        zero_state = jnp.zeros((nb, N_HEADS_B, 2, LANES, LANES), F32)
        oa, ob, oc = (jnp.zeros((n_lat + n_ctx, BRANCH_W), BF16) for _ in range(N_BRANCH))
        ob, state = _retention(proj, dec_tab, zero_state, ob, latent=False, **kw)
        ob, _ = _retention(proj, dec_tab, state, ob, latent=True, **kw)
        if ctx_out:
            oa = _gqa(proj, sink_tab, oa, local=False, **kw)
            oc = _diff_attention(proj, lam_p, gain, oc, lam_init=lam_init, latent=False, **kw)
        oa = _gqa(proj, sink_tab, oa, local=True, **kw)
        oc = _diff_attention(proj, lam_p, gain, oc, lam_init=lam_init, latent=True, **kw)

        is_moe = l % 2 == 1
        xs, tok = _merge(oa, ob, oc, proj, w_branch[l].astype(BF16), w_out[l].astype(BF16), xs, mod[l],
                         row(norm_ffn[l]), n_tiles=n_tiles, tok_dtype=F32 if is_moe else BF16, **tile_kw)
        final = l == depth - 1
        if is_moe:
            e = l // 2
            xs = _moe(tok, xs, mod[l], moe_router[e], *moe_w, row(norm_final), n_rows=n_tiles * tm, tm=tm,
                      n_lat_tiles=n_lat_tiles, nb=nb, s_len=s_len, final=final, first_expert=e * N_EXPERTS)
        else:
            e = l // 2
            xs = _ffn(tok, *ffn_w, one_block + e, jnp.full((1,), n_tiles, I32), n_blocks=n_tiles, xs=xs,
                      mod_l=mod[l], **tile_kw)
            assert not final
    return xs[:n_lat].reshape(nb, s_len, d)
```
